```python
import math
import jax, jax.numpy as jnp
from jax import lax
import numpy as np

D_MODEL = 1024
BATCH = 8
SEQ = 8192
DEPTH = 4
DEC_BATCH = 16
DEC_SEQ = 4096
PAST_LEN = 128

N_EVEN = (DEPTH + 1) // 2
N_ODD = DEPTH // 2
HEAD_DIM = 64
EPS = 1e-6
NEG = -1e30
A_HEADS = D_MODEL // 128
A_WIDTH = A_HEADS * HEAD_DIM
A_CONV = 5
MLSTM_CHUNK = 128
B_HEADS = D_MODEL // 256
B_WIDTH = B_HEADS * 2 * HEAD_DIM
B_QBLOCK = 128
C_PAIRS = ((128, 1), (512, 4), (2048, 16))
C_N_GROUPS = len(C_PAIRS)
C_HPG = D_MODEL // 256
C_HEADS = C_N_GROUPS * C_HPG
C_WIDTH = C_HEADS * HEAD_DIM
C_BLOCK = 64
D_HEADS = D_MODEL // 128
D_KV_HEADS = D_HEADS // 4
D_WIDTH = D_HEADS * HEAD_DIM
D_KV_WIDTH = D_KV_HEADS * HEAD_DIM
D_HALF_WINDOW = 128
D_BLOCK = 128
D_FF = 4 * D_MODEL
N_BUCKETS = 32
MAX_DISTANCE = 1024
B_COL0 = 0
C_COL0 = B_HEADS
D_COL0 = B_HEADS + C_HEADS
N_BIAS_COLS = D_COL0 + D_HEADS
EVEN_SIZES = (2 * A_WIDTH, A_WIDTH, A_WIDTH, 4 * A_HEADS, B_WIDTH, B_WIDTH, B_WIDTH)
EVEN_IN = sum(EVEN_SIZES)
EVEN_OUT_IN = A_WIDTH + B_WIDTH
ODD_SIZES = (C_WIDTH, C_WIDTH, C_WIDTH, D_WIDTH, D_KV_WIDTH, D_KV_WIDTH)
ODD_IN = sum(ODD_SIZES)
ODD_OUT_IN = C_WIDTH + D_WIDTH

kernel_name = "hybrid_mlstm_diffattn_dilated_swa_encoder"


def _split(t, sizes):
    idx = [int(i) for i in np.cumsum(sizes)[:-1]]
    return jnp.split(t, idx, axis=-1)


def rms_norm(x, g):
    x32 = x.astype(jnp.float32)
    y = x32 * lax.rsqrt(jnp.mean(x32 * x32, axis=-1, keepdims=True) + EPS)
    return (y * g.astype(jnp.float32)).astype(x.dtype)


def rel_bucket(rel):
    half = N_BUCKETS // 2
    max_exact = half // 2
    ret = jnp.where(rel > 0, half, 0)
    n = jnp.abs(rel)
    nf = jnp.maximum(n, 1).astype(jnp.float32)
    large = max_exact + (jnp.log(nf / max_exact) / math.log(MAX_DISTANCE / max_exact)
                         * (half - max_exact)).astype(jnp.int32)
    large = jnp.minimum(large, half - 1)
    return ret + jnp.where(n < max_exact, n, large)


def rel_bias(table_cols, rel):
    return table_cols[rel_bucket(rel)].astype(jnp.float32)


def mlstm_scan(q, k, v, i_pre, logf):
    bsz, nh, s, dh = q.shape
    nc = s // MLSTM_CHUNK

    def to_chunks(t):
        t = t.astype(jnp.float32)
        return jnp.moveaxis(t.reshape(t.shape[:2] + (nc, MLSTM_CHUNK) + t.shape[3:]), 2, 0)

    tril = jnp.tril(jnp.ones((MLSTM_CHUNK, MLSTM_CHUNK), dtype=bool))

    def step(carry, xs):
        c_st, n_st, m_st = carry
        qc, kc, vc, ic, fc = xs
        b = jnp.cumsum(fc, axis=-1)
        d_log = jnp.where(tril, b[..., :, None] - b[..., None, :] + ic[..., None, :], -jnp.inf)
        inter = b + m_st[..., None]
        m_t = jnp.maximum(inter, jnp.max(d_log, axis=-1))
        w_intra = jnp.exp(d_log - m_t[..., None])
        w_inter = jnp.exp(inter - m_t)
        sc = jnp.einsum('bhqd,bhkd->bhqk', qc, kc) * w_intra
        num = (jnp.einsum('bhqk,bhke->bhqe', sc, vc)
               + w_inter[..., None] * jnp.einsum('bhqd,bhde->bhqe', qc, c_st))
        den = jnp.sum(sc, axis=-1) + w_inter * jnp.einsum('bhqd,bhd->bhq', qc, n_st)
        h = num / jnp.maximum(jnp.abs(den), jnp.exp(-m_t))[..., None]
        b_last = b[..., -1]
        g = b_last[..., None] - b + ic
        m_new = jnp.maximum(b_last + m_st, jnp.max(g, axis=-1))
        decay = jnp.exp(b_last + m_st - m_new)
        wk = jnp.exp(g - m_new[..., None])[..., None] * kc
        c_new = decay[..., None, None] * c_st + jnp.einsum('bhld,bhle->bhde', wk, vc)
        n_new = decay[..., None] * n_st + jnp.sum(wk, axis=2)
        return (c_new, n_new, m_new), h

    init = (jnp.zeros((bsz, nh, dh, dh), jnp.float32),
            jnp.zeros((bsz, nh, dh), jnp.float32),
            jnp.zeros((bsz, nh), jnp.float32))
    xs = (to_chunks(q), to_chunks(k), to_chunks(v), to_chunks(i_pre), to_chunks(logf))
    _, hs = lax.scan(step, init, xs)
    return jnp.moveaxis(hs, 0, 2).reshape(bsz, nh, s, dh)


def diff_attention(q, k, v, lam, g, table_cols, layer_idx):
    bsz, s, _ = q.shape
    q = q.reshape(bsz, s, B_HEADS, 2, HEAD_DIM)
    k = k.reshape(bsz, s, B_HEADS, 2, HEAD_DIM)
    v = v.reshape(bsz, s, B_HEADS, 2 * HEAD_DIM)
    scale = HEAD_DIM ** -0.5
    lam_init = 0.8 - 0.6 * math.exp(-0.3 * layer_idx)
    lam32 = lam.astype(jnp.float32)
    lam_full = (jnp.exp(jnp.sum(lam32[0] * lam32[1])) - jnp.exp(jnp.sum(lam32[2] * lam32[3]))
                + lam_init)
    nq = s // B_QBLOCK
    q_blocks = q.reshape(bsz, nq, B_QBLOCK, B_HEADS, 2, HEAD_DIM).transpose(1, 0, 2, 3, 4, 5)
    k_pos = jnp.arange(s)

    def one_block(args):
        qi, idx = args
        q_pos = idx * B_QBLOCK + jnp.arange(B_QBLOCK)
        bias = rel_bias(table_cols, k_pos[None, :] - q_pos[:, None]).transpose(2, 0, 1)
        sc = jnp.einsum('bqhtd,bkhtd->bthqk', qi, k).astype(jnp.float32) * scale + bias[None, None]
        p = jax.nn.softmax(sc, axis=-1)
        a = p[:, 0] - lam_full * p[:, 1]
        return jnp.einsum('bhqk,bkhe->bqhe', a.astype(v.dtype), v)

    o = lax.map(one_block, (q_blocks, jnp.arange(nq)))
    o = o.transpose(1, 0, 2, 3, 4).reshape(bsz, s, B_HEADS, 2 * HEAD_DIM)
    o = rms_norm(o, g.reshape(B_HEADS, 2 * HEAD_DIM)) * (1.0 - lam_init)
    return o.reshape(bsz, s, B_WIDTH)


def local_attention(q, k, v, half_w, blk, dil, table_cols, sink):
    n, seq_len, nh, dh = q.shape
    ng = k.shape[2]
    nr = nh // ng
    nb = -(-half_w // blk)
    nblk = -(-seq_len // blk)
    lp = nblk * blk
    width = (2 * nb + 1) * blk
    qp = jnp.pad(q, ((0, 0), (0, lp - seq_len), (0, 0), (0, 0))).reshape(n, nblk, blk, ng, nr, dh)
    pad_kv = ((0, 0), (nb * blk, lp - seq_len + nb * blk), (0, 0), (0, 0))
    kp = jnp.pad(k, pad_kv).reshape(n, nblk + 2 * nb, blk, ng, dh)
    vp = jnp.pad(v, pad_kv).reshape(n, nblk + 2 * nb, blk, ng, dh)
    kw = jnp.concatenate([kp[:, j:j + nblk] for j in range(2 * nb + 1)], axis=2)
    vw = jnp.concatenate([vp[:, j:j + nblk] for j in range(2 * nb + 1)], axis=2)
    delta = jnp.arange(width)[None, :] - nb * blk - jnp.arange(blk)[:, None]
    k_pos = (jnp.arange(nblk)[:, None] - nb) * blk + jnp.arange(width)[None, :]
    mask = (jnp.abs(delta) <= half_w)[None] & ((k_pos >= 0) & (k_pos < seq_len))[:, None, :]
    bias = rel_bias(table_cols, delta * dil).reshape(blk, width, ng, nr).transpose(2, 3, 0, 1)
    sc = jnp.einsum('nibgrd,nicgd->nigrbc', qp, kw).astype(jnp.float32) * (dh ** -0.5) + bias
    sc = jnp.where(mask[None, :, None, None], sc, NEG)
    m = jnp.max(sc, axis=-1)
    if sink is not None:
        sk = sink.astype(jnp.float32).reshape(ng, nr)[None, None, :, :, None]
        m = jnp.maximum(m, sk)
    p = jnp.exp(sc - m[..., None])
    den = jnp.sum(p, axis=-1)
    if sink is not None:
        den = den + jnp.exp(sk - m)
    o = jnp.einsum('nigrbc,nicgd->nibgrd', (p / den[..., None]).astype(v.dtype), vw)
    o = o.reshape(n, lp, nh, dh)[:, :seq_len]
    lse = (m + jnp.log(den)).transpose(0, 1, 4, 2, 3).reshape(n, lp, nh)[:, :seq_len]
    return o, lse


def even_mixer(h, layer_idx, w_in, w_out, conv_w, conv_b, gate_b, mlstm_g, lam, diff_g, bias_table):
    bsz, s, _ = h.shape
    qk_a, v_a, o_a, gates, q_b, k_b, v_b = _split(h @ w_in, EVEN_SIZES)
    qk_a = lax.conv_general_dilated(qk_a, conv_w[:, None, :].astype(qk_a.dtype), window_strides=(1,),
                                    padding=((A_CONV // 2, A_CONV // 2),),
                                    dimension_numbers=('NWC', 'WIO', 'NWC'),
                                    feature_group_count=2 * A_WIDTH) + conv_b
    qk_a = jax.nn.silu(qk_a)
    q_a, k_a = jnp.split(qk_a, 2, axis=-1)

    def heads(t):
        return t.reshape(bsz, s, A_HEADS, HEAD_DIM).transpose(0, 2, 1, 3)

    q_a, k_a, v_a = heads(q_a), heads(k_a) * (HEAD_DIM ** -0.5), heads(v_a)
    g4 = (gates.reshape(bsz, s, 4, A_HEADS).astype(jnp.float32) + gate_b.astype(jnp.float32))
    i_f, f_f, i_b, f_b = g4.transpose(2, 0, 3, 1)
    h_fwd = mlstm_scan(q_a, k_a, v_a, i_f, jax.nn.log_sigmoid(f_f))

    def flip(t):
        return jnp.flip(t, axis=2)

    h_bwd = flip(mlstm_scan(flip(q_a), flip(k_a), flip(v_a), flip(i_b), flip(jax.nn.log_sigmoid(f_b))))
    h_a = (h_fwd + h_bwd).transpose(0, 2, 1, 3)
    h_a = rms_norm(h_a, mlstm_g.reshape(A_HEADS, HEAD_DIM)).reshape(bsz, s, A_WIDTH).astype(h.dtype)
    h_a = jax.nn.sigmoid(o_a) * h_a
    h_b = diff_attention(q_b, k_b, v_b, lam, diff_g,
                         bias_table[:, B_COL0:B_COL0 + B_HEADS], layer_idx)
    return jnp.concatenate([h_a, h_b], axis=-1) @ w_out


def odd_mixer(h, w_in, w_out, sink, bias_table):
    bsz, s, _ = h.shape
    q_c, k_c, v_c, q_d, k_d, v_d = _split(h @ w_in, ODD_SIZES)
    q_c = q_c.reshape(bsz, s, C_N_GROUPS, C_HPG, HEAD_DIM)
    k_c = k_c.reshape(bsz, s, C_N_GROUPS, C_HPG, HEAD_DIM)
    v_c = v_c.reshape(bsz, s, C_N_GROUPS, C_HPG, HEAD_DIM)
    outs, lses = [], []
    for g, (window, dil) in enumerate(C_PAIRS):
        sub = s // dil

        def to_strided(t):
            return t.reshape(bsz, sub, dil, C_HPG, HEAD_DIM).transpose(0, 2, 1, 3, 4).reshape(
                bsz * dil, sub, C_HPG, HEAD_DIM)

        cols = bias_table[:, C_COL0 + g * C_HPG:C_COL0 + (g + 1) * C_HPG]
        o_g, lse_g = local_attention(to_strided(q_c[:, :, g]), to_strided(k_c[:, :, g]),
                                     to_strided(v_c[:, :, g]), window // (2 * dil), C_BLOCK, dil,
                                     cols, None)
        outs.append(o_g.reshape(bsz, dil, sub, C_HPG, HEAD_DIM).transpose(0, 2, 1, 3, 4).reshape(
            bsz, s, C_HPG, HEAD_DIM))
        lses.append(lse_g.reshape(bsz, dil, sub, C_HPG).transpose(0, 2, 1, 3).reshape(bsz, s, C_HPG))
    w = jax.nn.softmax(jnp.stack(lses, axis=2), axis=2)
    o_c = (jnp.stack(outs, axis=2) * w[..., None].astype(h.dtype)).reshape(bsz, s, C_WIDTH)
    o_d, _ = local_attention(q_d.reshape(bsz, s, D_HEADS, HEAD_DIM),
                             k_d.reshape(bsz, s, D_KV_HEADS, HEAD_DIM),
                             v_d.reshape(bsz, s, D_KV_HEADS, HEAD_DIM),
                             D_HALF_WINDOW, D_BLOCK, 1,
                             bias_table[:, D_COL0:D_COL0 + D_HEADS], sink)
    return jnp.concatenate([o_c, o_d.reshape(bsz, s, D_WIDTH)], axis=-1) @ w_out


def squared_relu_mlp(h, w1, w2):
    return jnp.square(jax.nn.relu(h @ w1)) @ w2


def trunk(x, c, p):
    for l in range(DEPTH):
        mod = (jax.nn.silu(c) @ p['ada_w'][l] + p['ada_b'][l])[:, None, :]
        sh1, sc1, g1, sh2, sc2, g2 = jnp.split(mod, 6, axis=-1)
        h = rms_norm(x, p['mix_pre_g'][l]) * (1.0 + sc1) + sh1
        if l % 2 == 0:
            e = l // 2
            y = even_mixer(h, l, p['even_w_in'][e], p['even_w_out'][e], p['even_conv_w'][e],
                           p['even_conv_b'][e], p['mlstm_gate_b'][e], p['mlstm_norm_g'][e],
                           p['diff_lambda'][e], p['diff_norm_g'][e], p['rel_bias_table'])
        else:
            o = l // 2
            y = odd_mixer(h, p['odd_w_in'][o], p['odd_w_out'][o], p['sink_logit'][o],
                          p['rel_bias_table'])
        x = x + g1 * rms_norm(y, p['mix_post_g'][l])
        h = rms_norm(x, p['mlp_pre_g'][l]) * (1.0 + sc2) + sh2
        y = squared_relu_mlp(h, p['mlp_w1'][l], p['mlp_w2'][l])
        x = x + g2 * rms_norm(y, p['mlp_post_g'][l])
    return x


def setup_inputs(seed: int = 0) -> dict:
    key = jax.random.key(seed)
    ki = iter(jax.random.split(key, 40))

    def nrm(shape, s):
        return jax.random.normal(next(ki), shape, jnp.float32) * s

    def gain(shape):
        return 1.0 + 0.02 * jax.random.normal(next(ki), shape, jnp.float32)

    def fgate_bias():
        return 3.0 + 3.0 * jax.random.uniform(next(ki), (N_EVEN, A_HEADS), jnp.float32)

    gate_b = jnp.stack([nrm((N_EVEN, A_HEADS), 0.1), fgate_bias(),
                        nrm((N_EVEN, A_HEADS), 0.1), fgate_bias()], axis=1)
    return {
        "x_prompt": nrm((BATCH, SEQ, D_MODEL), 1.0),
        "x_sample": nrm((DEC_BATCH, DEC_SEQ, D_MODEL), 1.0),
        "c_prompt": nrm((BATCH, D_MODEL), 1.0),
        "c_sample": nrm((DEC_BATCH, D_MODEL), 1.0),
        "rel_bias_table": nrm((N_BUCKETS, N_BIAS_COLS), 0.5),
        "ada_w": nrm((DEPTH, D_MODEL, 6 * D_MODEL), 0.5 * D_MODEL ** -0.5),
        "ada_b": nrm((DEPTH, 6 * D_MODEL), 0.02),
        "mix_pre_g": gain((DEPTH, D_MODEL)),
        "mix_post_g": gain((DEPTH, D_MODEL)),
        "mlp_pre_g": gain((DEPTH, D_MODEL)),
        "mlp_post_g": gain((DEPTH, D_MODEL)),
        "mlp_w1": nrm((DEPTH, D_MODEL, D_FF), D_MODEL ** -0.5),
        "mlp_w2": nrm((DEPTH, D_FF, D_MODEL), D_FF ** -0.5),
        "even_w_in": nrm((N_EVEN, D_MODEL, EVEN_IN), D_MODEL ** -0.5),
        "even_conv_w": nrm((N_EVEN, A_CONV, 2 * A_WIDTH), A_CONV ** -0.5),
        "even_conv_b": nrm((N_EVEN, 2 * A_WIDTH), 0.02),
        "mlstm_gate_b": gate_b,
        "mlstm_norm_g": gain((N_EVEN, A_WIDTH)),
        "diff_lambda": nrm((N_EVEN, 4, HEAD_DIM), 0.1),
        "diff_norm_g": gain((N_EVEN, B_WIDTH)),
        "even_w_out": nrm((N_EVEN, EVEN_OUT_IN, D_MODEL), EVEN_OUT_IN ** -0.5),
        "odd_w_in": nrm((N_ODD, D_MODEL, ODD_IN), D_MODEL ** -0.5),
        "odd_w_out": nrm((N_ODD, ODD_OUT_IN, D_MODEL), ODD_OUT_IN ** -0.5),
        "sink_logit": nrm((N_ODD, D_HEADS), 0.5),
    }


def reference(x_prompt, x_sample, c_prompt, c_sample, rel_bias_table, ada_w, ada_b,
              mix_pre_g, mix_post_g, mlp_pre_g, mlp_post_g, mlp_w1, mlp_w2,
              even_w_in, even_conv_w, even_conv_b, mlstm_gate_b, mlstm_norm_g,
              diff_lambda, diff_norm_g, even_w_out, odd_w_in, odd_w_out, sink_logit):
    params = dict(rel_bias_table=rel_bias_table, ada_w=ada_w, ada_b=ada_b,
                  mix_pre_g=mix_pre_g, mix_post_g=mix_post_g, mlp_pre_g=mlp_pre_g,
                  mlp_post_g=mlp_post_g, mlp_w1=mlp_w1, mlp_w2=mlp_w2,
                  even_w_in=even_w_in, even_conv_w=even_conv_w, even_conv_b=even_conv_b,
                  mlstm_gate_b=mlstm_gate_b, mlstm_norm_g=mlstm_norm_g,
                  diff_lambda=diff_lambda, diff_norm_g=diff_norm_g, even_w_out=even_w_out,
                  odd_w_in=odd_w_in, odd_w_out=odd_w_out, sink_logit=sink_logit)
    y_prompt = trunk(x_prompt, c_prompt, params)
    y_sample = trunk(x_sample, c_sample, params)
    return (y_prompt, y_sample)
```

```python
import functools
import math

import jax
import jax.numpy as jnp
from jax import lax
from jax.experimental import pallas as pl
from jax.experimental.pallas import tpu as pltpu

F32 = jnp.float32
BF16 = jnp.bfloat16

D_MODEL = 1024
DEPTH = 4
HEAD_DIM = 64
PAIR = 2 * HEAD_DIM
EPS = 1e-6
NEG = -1e30
QK_SCALE = HEAD_DIM ** -0.5
A_HEADS = D_MODEL // 128
A_WIDTH = A_HEADS * HEAD_DIM
A_CONV = 5
A_CHUNK = 128
N_GATES = 4 * A_HEADS
B_HEADS = D_MODEL // 256
B_WIDTH = B_HEADS * 2 * HEAD_DIM
C_PAIRS = ((128, 1), (512, 4), (2048, 16))
C_HPG = D_MODEL // 256
C_GW = C_HPG * HEAD_DIM
C_WIDTH = len(C_PAIRS) * C_GW
D_HEADS = D_MODEL // 128
D_KV_HEADS = D_HEADS // 4
D_WIDTH = D_HEADS * HEAD_DIM
D_HALF_WINDOW = 128
D_FF = 4 * D_MODEL
N_BUCKETS = 32
MAX_DISTANCE = 1024
B_COL0 = 0
C_COL0 = B_HEADS
D_COL0 = B_HEADS + len(C_PAIRS) * C_HPG

VMEM_LIMIT = 56 * 1024 * 1024
CONV_HALO = 16
TOKEN_TILE = 512
DIFF_TQ = 256
DIFF_TK = 512
LOCAL_TILE = 128
LOCAL_OUTER = 512


def _cparams(*sem):
    return pltpu.CompilerParams(dimension_semantics=sem, vmem_limit_bytes=VMEM_LIMIT)


def _dot(a, b):
    return jnp.dot(a, b, preferred_element_type=F32)


def _dot_nt(a, b):
    return lax.dot_general(a, b, (((1,), (1,)), ((), ())), preferred_element_type=F32)


def _dot_tn(a, b):
    return lax.dot_general(a, b, (((0,), (0,)), ((), ())), preferred_element_type=F32)


def _rms(x, g):
    return x * lax.rsqrt(jnp.mean(x * x, axis=-1, keepdims=True) + EPS) * g


def _const_spec(shape):
    zeros = (0,) * len(shape)
    return pl.BlockSpec(shape, lambda *_: zeros)


def _rel_bucket(rel):
    half = N_BUCKETS // 2
    max_exact = half // 2
    ret = jnp.where(rel > 0, half, 0)
    n = jnp.abs(rel)
    nf = jnp.maximum(n, 1).astype(F32)
    large = max_exact + (jnp.log(nf / max_exact) / math.log(MAX_DISTANCE / max_exact)
                         * (half - max_exact)).astype(jnp.int32)
    large = jnp.minimum(large, half - 1)
    return ret + jnp.where(n < max_exact, n, large)


def _bias_tile_kernel(table_ref, out_ref, *, col0, offset0, stride, dil, band):
    h = pl.program_id(0)
    t = pl.program_id(1)
    tq, tk = out_ref.shape[2:]
    qi = lax.broadcasted_iota(jnp.int32, (tq, tk), 0)
    ki = lax.broadcasted_iota(jnp.int32, (tq, tk), 1)
    delta = offset0 + t * stride + ki - qi
    bucket = _rel_bucket(delta * dil)
    val = jnp.zeros((tq, tk), F32)
    for j in range(N_BUCKETS):
        val = jnp.where(bucket == j, table_ref[j, col0 + h], val)
    if band is not None:
        val = jnp.where(jnp.abs(delta) <= band, val, NEG)
    out_ref[0, 0] = val


def _bias_tiles(table, *, n_heads, n_tiles, tq, tk, col0, offset0, stride, dil, band):
    return pl.pallas_call(
        functools.partial(_bias_tile_kernel, col0=col0, offset0=offset0, stride=stride, dil=dil,
                          band=band),
        grid=(n_heads, n_tiles),
        in_specs=[pl.BlockSpec(memory_space=pltpu.SMEM)],
        out_specs=pl.BlockSpec((1, 1, tq, tk), lambda h, t: (h, t, 0, 0)),
        out_shape=jax.ShapeDtypeStruct((n_heads, n_tiles, tq, tk), F32),
        compiler_params=_cparams("parallel", "parallel"),
        name="rel_bias_tiles",
    )(table)


def _mod_kernel(c_ref, w_ref, b_ref, o_ref):
    c = c_ref[...]
    a = (c * jax.nn.sigmoid(c)).astype(BF16)
    o_ref[0] = _dot(a, w_ref[0].astype(BF16)) + b_ref[0]


def _modulation(c_all, ada_w, ada_b):
    n, d = c_all.shape
    depth, _, width = ada_w.shape
    tn = 1536
    return pl.pallas_call(
        _mod_kernel,
        grid=(depth, width // tn),
        in_specs=[_const_spec((n, d)),
                  pl.BlockSpec((1, d, tn), lambda l, j: (l, 0, j)),
                  pl.BlockSpec((1, 1, tn), lambda l, j: (l, 0, j))],
        out_specs=pl.BlockSpec((1, n, tn), lambda l, j: (l, 0, j)),
        out_shape=jax.ShapeDtypeStruct((depth, n, width), F32),
        compiler_params=_cparams("parallel", "parallel"),
        name="adaln_modulation",
    )(c_all, ada_w, ada_b.reshape(depth, 1, width))


def _prenorm(x, g, shift, scale):
    return (_rms(x, g) * (1.0 + scale) + shift).astype(BF16)


def _even_in_kernel(x_ref, xp_ref, xn_ref, mod_ref, g_ref, wqk_ref, wvo_ref, wg_ref, wb_ref,
                    cw_ref, cb_ref, gb_ref,
                    qa_ref, ka_ref, va_ref, oa_ref, gates_ref, qb_ref, kb_ref, vb_ref, ext_ref):
    i = pl.program_id(1)
    n = pl.num_programs(1)
    tm = x_ref.shape[1]
    g = g_ref[...]
    shift = mod_ref[0, 0:1, :]
    scale = mod_ref[0, 1:2, :]
    h = _prenorm(x_ref[0], g, shift, scale)
    wqk = wqk_ref[...]
    prev_ok = (i > 0).astype(F32)
    next_ok = (i < n - 1).astype(F32)
    ext_ref[0:CONV_HALO, :] = _dot(_prenorm(xp_ref[0], g, shift, scale), wqk) * prev_ok
    ext_ref[CONV_HALO:CONV_HALO + tm, :] = _dot(h, wqk)
    ext_ref[CONV_HALO + tm:, :] = _dot(_prenorm(xn_ref[0], g, shift, scale), wqk) * next_ok
    acc = jnp.broadcast_to(cb_ref[...], (tm, 2 * A_WIDTH))
    for j in range(A_CONV):
        start = CONV_HALO - A_CONV // 2 + j
        acc = acc + cw_ref[j:j + 1, :] * ext_ref[start:start + tm, :]
    qk = acc * jax.nn.sigmoid(acc)
    qa_ref[0] = qk[:, :A_WIDTH].astype(BF16)
    ka_ref[0] = (qk[:, A_WIDTH:] * QK_SCALE).astype(BF16)
    vo = _dot(h, wvo_ref[...])
    va_ref[0] = vo[:, :A_WIDTH].astype(BF16)
    oa_ref[0] = vo[:, A_WIDTH:].astype(BF16)
    gates_ref[0] = _dot(h, wg_ref[...]) + gb_ref[...]
    qkv = _dot(h, wb_ref[...])
    qb_ref[0] = (qkv[:, :B_WIDTH] * QK_SCALE).astype(BF16)
    kb_ref[0] = qkv[:, B_WIDTH:2 * B_WIDTH].astype(BF16)
    vb_ref[0] = qkv[:, 2 * B_WIDTH:].astype(BF16)


def _even_in_proj(x, mod, g, w, conv_w, conv_b, gate_b):
    bsz, s, d = x.shape
    tm = min(TOKEN_TILE, s)
    nt = s // tm
    hb = tm // CONV_HALO
    n_halo = s // CONV_HALO
    tok = lambda width: pl.BlockSpec((1, tm, width), lambda b, i: (b, i, 0))
    outs = [A_WIDTH, A_WIDTH, A_WIDTH, A_WIDTH, N_GATES, B_WIDTH, B_WIDTH, B_WIDTH]
    dts = [BF16, BF16, BF16, BF16, F32, BF16, BF16, BF16]
    return pl.pallas_call(
        _even_in_kernel,
        grid=(bsz, nt),
        in_specs=[tok(d),
                  pl.BlockSpec((1, CONV_HALO, d), lambda b, i: (b, jnp.maximum(i * hb - 1, 0), 0)),
                  pl.BlockSpec((1, CONV_HALO, d),
                               lambda b, i: (b, jnp.minimum((i + 1) * hb, n_halo - 1), 0)),
                  pl.BlockSpec((1, 6, d), lambda b, i: (b, 0, 0)),
                  _const_spec((1, d)),
                  _const_spec(w["qk"].shape), _const_spec(w["vo"].shape),
                  _const_spec(w["g"].shape), _const_spec(w["b"].shape),
                  _const_spec(conv_w.shape), _const_spec(conv_b.shape), _const_spec(gate_b.shape)],
        out_specs=[tok(width) for width in outs],
        out_shape=[jax.ShapeDtypeStruct((bsz, s, width), dt) for width, dt in zip(outs, dts)],
        scratch_shapes=[pltpu.VMEM((tm + 2 * CONV_HALO, 2 * A_WIDTH), F32)],
        compiler_params=_cparams("parallel", "parallel"),
        name="even_in_proj",
    )(x, x, x, mod, g, w["qk"], w["vo"], w["g"], w["b"], conv_w, conv_b, gate_b)


def _odd_in_kernel(x_ref, mod_ref, g_ref, wc_ref, wd_ref, c0_ref, c1_ref, c2_ref, qd_ref, kd_ref, vd_ref):
    h = _prenorm(x_ref[0], g_ref[...], mod_ref[0, 0:1, :], mod_ref[0, 1:2, :])
    for g, ref in enumerate((c0_ref, c1_ref, c2_ref)):
        ref[0] = _dot(h, wc_ref[g]).astype(BF16)
    d = _dot(h, wd_ref[...])
    qd_ref[0] = d[:, :D_WIDTH].astype(BF16)
    kd_ref[0] = d[:, D_WIDTH:D_WIDTH + 2 * PAIR].astype(BF16)
    vd_ref[0] = d[:, D_WIDTH + 2 * PAIR:].astype(BF16)


def _odd_in_proj(x, mod, g, w):
    bsz, s, d = x.shape
    tm = min(TOKEN_TILE, s)
    tok = lambda width: pl.BlockSpec((1, tm, width), lambda b, i: (b, i, 0))
    outs = [3 * C_GW] * 3 + [D_WIDTH, 2 * PAIR, 2 * PAIR]
    return pl.pallas_call(
        _odd_in_kernel,
        grid=(bsz, s // tm),
        in_specs=[tok(d), pl.BlockSpec((1, 6, d), lambda b, i: (b, 0, 0)), _const_spec((1, d)),
                  _const_spec(w["c"].shape), _const_spec(w["d"].shape)],
        out_specs=[tok(width) for width in outs],
        out_shape=[jax.ShapeDtypeStruct((bsz, s, width), BF16) for width in outs],
        compiler_params=_cparams("parallel", "parallel"),
        name="odd_in_proj",
    )(x, mod, g, w["c"], w["d"])


def _log_sigmoid(x):
    return jnp.minimum(x, 0.0) - jnp.log1p(jnp.exp(-jnp.abs(x)))


def _mlstm_kernel(*refs, reverse):
    if reverse:
        (q_ref, k_ref, v_ref, gc_ref, gr_ref, hf_ref, oa_ref, ng_ref, out_ref, c_scr, m_scr) = refs
    else:
        (q_ref, k_ref, v_ref, gc_ref, gr_ref, out_ref, c_scr, m_scr) = refs
    L = q_ref.shape[1]

    @pl.when(pl.program_id(1) == 0)
    def _():
        c_scr[...] = jnp.zeros_like(c_scr)
        m_scr[...] = jnp.zeros_like(m_scr)

    row = lax.broadcasted_iota(jnp.int32, (L, L), 0)
    col = lax.broadcasted_iota(jnp.int32, (L, L), 1)
    causal = (col >= row) if reverse else (col <= row)
    causal_t = (row >= col) if reverse else (row <= col)
    logf_c = _log_sigmoid(gc_ref[0])
    logf_r = _log_sigmoid(gr_ref[0])
    cum_c = jnp.dot(causal.astype(F32), logf_c, precision=lax.Precision.HIGHEST,
                    preferred_element_type=F32)
    cum_r = jnp.dot(logf_r, causal_t.astype(F32), precision=lax.Precision.HIGHEST,
                    preferred_element_type=F32)
    i_col0 = 2 * A_HEADS if reverse else 0
    f_col0 = i_col0 + A_HEADS
    last = 0 if reverse else L - 1
    lane = lax.broadcasted_iota(jnp.int32, (L, PAIR), 1)
    low = lane < HEAD_DIM

    for p in range(A_HEADS // 2):
        cols = slice(p * PAIR, (p + 1) * PAIR)
        q2, k2, v2 = q_ref[0, :, cols], k_ref[0, :, cols], v_ref[0, :, cols]
        halves = []
        for half in range(2):
            hd = 2 * p + half
            mine = low if half == 0 else jnp.logical_not(low)
            ones_lane = lane == (HEAD_DIM if half == 0 else 0)
            qm = jnp.where(mine, q2, jnp.zeros_like(q2))
            km = jnp.where(mine, k2, jnp.zeros_like(k2))
            v_ext = jnp.where(mine, v2, jnp.where(ones_lane, 1.0, 0.0).astype(BF16))
            i_c = gc_ref[0, :, i_col0 + hd:i_col0 + hd + 1]
            b_c = cum_c[:, f_col0 + hd:f_col0 + hd + 1]
            i_r = gr_ref[0, i_col0 + hd:i_col0 + hd + 1, :]
            b_r = cum_r[f_col0 + hd:f_col0 + hd + 1, :]
            r_c = i_c - b_c
            r_r = i_r - b_r
            m_prev = m_scr[hd, 0:1, 0:1]
            c_ext = c_scr[hd]
            d_log = jnp.where(causal, b_c + r_r, -jnp.inf)
            inter = b_c + m_prev
            m_t = jnp.maximum(inter, jnp.max(d_log, axis=1, keepdims=True))
            w_intra = jnp.exp(d_log - m_t)
            w_inter = jnp.exp(inter - m_t)
            sc = _dot_nt(qm, km) * w_intra
            tot = _dot(sc.astype(BF16), v_ext) + w_inter * _dot(qm, c_ext.astype(BF16))
            den = jnp.sum(jnp.where(ones_lane, tot, 0.0), axis=1, keepdims=True)
            halves.append(tot * (1.0 / jnp.maximum(jnp.abs(den), jnp.exp(-m_t))))
            b_tot = b_c[last:last + 1, :]
            m_new = jnp.maximum(b_tot + m_prev, jnp.max(b_tot + r_r, axis=1, keepdims=True))
            decay = jnp.exp(b_tot + m_prev - m_new)
            wk = (jnp.exp(b_tot + r_c - m_new) * km.astype(F32)).astype(BF16)
            c_scr[hd] = decay * c_ext + _dot_tn(wk, v_ext)
            m_scr[hd] = jnp.broadcast_to(m_new, m_scr.shape[1:])
        h2 = jnp.where(low, halves[0], halves[1])
        if not reverse:
            out_ref[0, :, cols] = h2
        else:
            hs = h2 + hf_ref[0, :, cols]
            sq = hs * hs
            ms0 = jnp.sum(jnp.where(low, sq, 0.0), axis=1, keepdims=True) / HEAD_DIM
            ms1 = jnp.sum(jnp.where(low, 0.0, sq), axis=1, keepdims=True) / HEAD_DIM
            rs = jnp.where(low, lax.rsqrt(ms0 + EPS), lax.rsqrt(ms1 + EPS))
            y = hs * rs * ng_ref[:, cols]
            out_ref[0, :, cols] = (jax.nn.sigmoid(oa_ref[0, :, cols].astype(F32)) * y).astype(BF16)


def _mlstm(q, k, v, gates, gates_t, *, reverse, h_fwd=None, o_gate=None, norm_g=None):
    bsz, s, _ = q.shape
    L = A_CHUNK
    nc = s // L
    pos = (lambda c: nc - 1 - c) if reverse else (lambda c: c)
    tok = lambda width: pl.BlockSpec((1, L, width), lambda b, c: (b, pos(c), 0))
    in_specs = [tok(A_WIDTH), tok(A_WIDTH), tok(A_WIDTH), tok(N_GATES),
                pl.BlockSpec((1, N_GATES, L), lambda b, c: (b, 0, pos(c)))]
    args = [q, k, v, gates, gates_t]
    if reverse:
        in_specs += [tok(A_WIDTH), tok(A_WIDTH), _const_spec((1, A_WIDTH))]
        args += [h_fwd, o_gate, norm_g]
    return pl.pallas_call(
        functools.partial(_mlstm_kernel, reverse=reverse),
        grid=(bsz, nc),
        in_specs=in_specs,
        out_specs=tok(A_WIDTH),
        out_shape=jax.ShapeDtypeStruct((bsz, s, A_WIDTH), BF16 if reverse else F32),
        scratch_shapes=[pltpu.VMEM((A_HEADS, PAIR, PAIR), F32), pltpu.VMEM((A_HEADS, 8, 128), F32)],
        compiler_params=_cparams("parallel", "arbitrary"),
        name="mlstm_bwd" if reverse else "mlstm_fwd",
    )(*args)


def _diff_tile_range(tq, tk):
    unit = math.gcd(tq, tk)
    e_hi = pl.cdiv(MAX_DISTANCE + tq - 1, unit)
    e_lo = -pl.cdiv(MAX_DISTANCE + tk - 1, unit)
    return unit, e_lo, e_hi


def _diff_kernel(lam_ref, q_ref, k_ref, v_ref, bias_ref, g_ref, o_ref, *, tk, unit, e_lo, e_hi,
                 lam_init):
    tq = q_ref.shape[1]
    s = k_ref.shape[1]
    q0 = pl.program_id(2) * tq
    lane = lax.broadcasted_iota(jnp.int32, (tq, PAIR), 1)
    q2 = q_ref[0]
    zero = jnp.zeros_like(q2)
    qq = jnp.concatenate([jnp.where(lane < HEAD_DIM, q2, zero),
                          jnp.where(lane < HEAD_DIM, zero, q2)], axis=0)

    def body(kb, carry):
        m, l, acc = carry
        k0 = pl.multiple_of(kb * tk, tk)
        k = k_ref[0, pl.ds(k0, tk), :]
        v = v_ref[0, pl.ds(k0, tk), :]
        e = jnp.clip((k0 - q0) // unit, e_lo, e_hi) - e_lo
        bias = bias_ref[0, e]
        sc = _dot_nt(qq, k) + jnp.concatenate([bias, bias], axis=0)
        m_new = jnp.maximum(m, jnp.max(sc, axis=1, keepdims=True))
        alpha = jnp.exp(m - m_new)
        p = jnp.exp(sc - m_new)
        l = alpha * l + jnp.sum(p, axis=1, keepdims=True)
        acc = alpha * acc + _dot(p.astype(BF16), v)
        return m_new, l, acc

    init = (jnp.full((2 * tq, 1), -jnp.inf, F32), jnp.zeros((2 * tq, 1), F32),
            jnp.zeros((2 * tq, PAIR), F32))
    _, l, acc = lax.fori_loop(0, s // tk, body, init)
    o = acc * (1.0 / l)
    lam = lam_ref[...]
    lam_full = (jnp.exp(jnp.sum(lam[0:1] * lam[1:2], axis=1, keepdims=True))
                - jnp.exp(jnp.sum(lam[2:3] * lam[3:4], axis=1, keepdims=True)) + lam_init)
    o = o[:tq] - lam_full * o[tq:]
    o_ref[0] = (_rms(o, g_ref[...]) * (1.0 - lam_init)).astype(BF16)


def _diff_attention(q, k, v, lam, g, bias_tiles, layer_idx):
    bsz, s, _ = q.shape
    tq, tk = min(DIFF_TQ, s), min(DIFF_TK, s)
    unit, e_lo, e_hi = _diff_tile_range(tq, tk)
    n_tiles = e_hi - e_lo + 1
    lam_init = 0.8 - 0.6 * math.exp(-0.3 * layer_idx)
    return pl.pallas_call(
        functools.partial(_diff_kernel, tk=tk, unit=unit, e_lo=e_lo, e_hi=e_hi, lam_init=lam_init),
        grid=(bsz, B_HEADS, s // tq),
        in_specs=[_const_spec(lam.shape),
                  pl.BlockSpec((1, tq, PAIR), lambda b, h, i: (b, i, h)),
                  pl.BlockSpec((1, s, PAIR), lambda b, h, i: (b, 0, h)),
                  pl.BlockSpec((1, s, PAIR), lambda b, h, i: (b, 0, h)),
                  pl.BlockSpec((1, n_tiles, tq, tk), lambda b, h, i: (h, 0, 0, 0)),
                  pl.BlockSpec((1, PAIR), lambda b, h, i: (0, h))],
        out_specs=pl.BlockSpec((1, tq, PAIR), lambda b, h, i: (b, i, h)),
        out_shape=jax.ShapeDtypeStruct((bsz, s, B_WIDTH), BF16),
        compiler_params=_cparams("parallel", "parallel", "arbitrary"),
        name="diff_attention",
    )(lam, q, k, v, bias_tiles, g)


def _diff_bias_tiles(table, s):
    tq, tk = min(DIFF_TQ, s), min(DIFF_TK, s)
    unit, e_lo, e_hi = _diff_tile_range(tq, tk)
    return _bias_tiles(table, n_heads=B_HEADS, n_tiles=e_hi - e_lo + 1, tq=tq, tk=tk, col0=B_COL0,
                       offset0=e_lo * unit, stride=unit, dil=1, band=None)


def _local_kernel(*refs, hw, ti, k_blocks, with_sink, with_lse):
    refs = list(refs)
    sink_ref = refs.pop(0) if with_sink else None
    (q_ref, kp_ref, kc_ref, kn_ref, vp_ref, vc_ref, vn_ref, bias_ref) = refs[:8]
    o_ref = refs[8]
    lse_ref = refs[9] if with_lse else None
    kext, vext = refs[-2:]
    tb = q_ref.shape[1]
    i = pl.program_id(2)
    seq = pl.num_programs(2) * tb
    kext[0:hw, :] = kp_ref[0]
    kext[hw:hw + tb, :] = kc_ref[0]
    kext[hw + tb:, :] = kn_ref[0]
    vext[0:hw, :] = vp_ref[0]
    vext[hw:hw + tb, :] = vc_ref[0]
    vext[hw + tb:, :] = vn_ref[0]
    wk = ti + 2 * hw
    lane = lax.broadcasted_iota(jnp.int32, (ti, PAIR), 1)
    low = lane < HEAD_DIM
    kj = lax.broadcasted_iota(jnp.int32, (ti, wk), 1)
    for j in range(tb // ti):
        rows = slice(j * ti, (j + 1) * ti)
        krows = slice(j * ti, j * ti + wk)
        kpos = i * tb + j * ti - hw + kj
        valid = (kpos >= 0) & (kpos < seq)
        for p, kb in enumerate(k_blocks):
            q2 = q_ref[0, rows, p * PAIR:(p + 1) * PAIR]
            kk = kext[krows, kb * PAIR:(kb + 1) * PAIR]
            vv = vext[krows, kb * PAIR:(kb + 1) * PAIR]
            outs, lses = [], []
            for half in range(2):
                hd = 2 * p + half
                mine = low if half == 0 else jnp.logical_not(low)
                qm = jnp.where(mine, q2, jnp.zeros_like(q2))
                sc = jnp.where(valid, _dot_nt(qm, kk) + bias_ref[hd], NEG)
                m = jnp.max(sc, axis=1, keepdims=True)
                if with_sink:
                    m = jnp.maximum(m, sink_ref[hd])
                pr = jnp.exp(sc - m)
                den = jnp.sum(pr, axis=1, keepdims=True)
                if with_sink:
                    den = den + jnp.exp(sink_ref[hd] - m)
                outs.append(_dot((pr * (1.0 / den)).astype(BF16), vv))
                lses.append(m + jnp.log(den))
            o_ref[0, rows, p * PAIR:(p + 1) * PAIR] = jnp.where(low, outs[0], outs[1]).astype(o_ref.dtype)
            if with_lse:
                lse_ref[0, rows, p * PAIR:(p + 1) * PAIR] = jnp.where(low, lses[0], lses[1])


def _local_attention(q_arr, k_arr, v_arr, bias, *, seq, n_res, q_width, q_col, k_col, v_col, hw,
                     k_blocks, out_dtype, sink=None, with_lse=False):
    bsz = q_arr.shape[0]
    tb = min(LOCAL_OUTER, seq)
    ti = min(LOCAL_TILE, tb)
    kw = len(set(k_blocks)) * PAIR
    hb = tb // hw
    n_halo = seq // hw
    cur = lambda width, colf: pl.BlockSpec((1, tb, width), lambda b, r, i: (b, i, colf(r)))
    prev = lambda colf: pl.BlockSpec((1, hw, kw), lambda b, r, i: (b, jnp.maximum(i * hb - 1, 0), colf(r)))
    nxt = lambda colf: pl.BlockSpec(
        (1, hw, kw), lambda b, r, i: (b, jnp.minimum((i + 1) * hb, n_halo - 1), colf(r)))
    in_specs = [cur(q_width, q_col), prev(k_col), cur(kw, k_col), nxt(k_col),
                prev(v_col), cur(kw, v_col), nxt(v_col), _const_spec(bias.shape)]
    args = [q_arr, k_arr, k_arr, k_arr, v_arr, v_arr, v_arr, bias]
    if sink is not None:
        in_specs.insert(0, pl.BlockSpec(memory_space=pltpu.SMEM))
        args.insert(0, sink)
    out_spec = pl.BlockSpec((1, tb, q_width), lambda b, r, i: (b, i, r))
    out_sds = jax.ShapeDtypeStruct((bsz, seq, n_res * q_width), out_dtype)
    out_specs, out_shape = [out_spec], [out_sds]
    if with_lse:
        out_specs.append(out_spec)
        out_shape.append(jax.ShapeDtypeStruct((bsz, seq, n_res * q_width), F32))
    return pl.pallas_call(
        functools.partial(_local_kernel, hw=hw, ti=ti, k_blocks=tuple(k_blocks),
                          with_sink=sink is not None, with_lse=with_lse),
        grid=(bsz, n_res, seq // tb),
        in_specs=in_specs,
        out_specs=out_specs,
        out_shape=out_shape,
        scratch_shapes=[pltpu.VMEM((tb + 2 * hw, kw), BF16), pltpu.VMEM((tb + 2 * hw, kw), BF16)],
        compiler_params=_cparams("parallel", "parallel", "parallel"),
        name="local_attention",
    )(*args)


def _post_kernel(*refs, even):
    refs = list(refs)
    if even:
        a_ref, b_ref = refs[:2]
        rest = refs[2:]
    else:
        c_refs = refs[:6]
        d_ref = refs[6]
        rest = refs[7:]
    x_ref, mod_ref, gains_ref, wo_ref, w1_ref, w2_ref, out_ref = rest
    if even:
        y = _dot(a_ref[0], wo_ref[0:A_WIDTH, :]) + _dot(b_ref[0], wo_ref[A_WIDTH:, :])
    else:
        os_ = [c_refs[2 * g][0] for g in range(3)]
        ls_ = [c_refs[2 * g + 1][0] for g in range(3)]
        mx = jnp.maximum(jnp.maximum(ls_[0], ls_[1]), ls_[2])
        es = [jnp.exp(l - mx) for l in ls_]
        inv = 1.0 / (es[0] + es[1] + es[2])
        y = _dot(d_ref[0], wo_ref[C_WIDTH:, :])
        for g in range(3):
            y = y + _dot((os_[g] * (es[g] * inv)).astype(BF16), wo_ref[g * C_GW:(g + 1) * C_GW, :])
    gate1 = mod_ref[0, 2:3, :]
    shift2 = mod_ref[0, 3:4, :]
    scale2 = mod_ref[0, 4:5, :]
    gate2 = mod_ref[0, 5:6, :]
    x = x_ref[0] + gate1 * _rms(y, gains_ref[0:1, :])
    h = _prenorm(x, gains_ref[1:2, :], shift2, scale2)
    acc = jnp.zeros_like(x)
    step = D_MODEL
    for j in range(D_FF // step):
        u = jnp.maximum(_dot(h, w1_ref[:, j * step:(j + 1) * step]), 0.0)
        acc = acc + _dot((u * u).astype(BF16), w2_ref[j * step:(j + 1) * step, :])
    out_ref[0] = x + gate2 * _rms(acc, gains_ref[2:3, :])


def _post_mixer(mix, x, mod, gains, wo, w1, w2, *, even):
    bsz, s, d = x.shape
    tm = min(TOKEN_TILE, s)
    tok = lambda width: pl.BlockSpec((1, tm, width), lambda b, i: (b, i, 0))
    resident = lambda a: pl.BlockSpec(a.shape, lambda b, i: (0, 0), pipeline_mode=pl.Buffered(1))
    in_specs = [tok(m.shape[-1]) for m in mix]
    in_specs += [tok(d), pl.BlockSpec((1, 6, d), lambda b, i: (b, 0, 0)), _const_spec(gains.shape),
                 resident(wo), resident(w1), resident(w2)]
    return pl.pallas_call(
        functools.partial(_post_kernel, even=even),
        grid=(bsz, s // tm),
        in_specs=in_specs,
        out_specs=tok(d),
        out_shape=jax.ShapeDtypeStruct((bsz, s, d), F32),
        compiler_params=_cparams("parallel", "parallel"),
        name="post_mixer_mlp",
    )(*mix, x, mod, gains, wo, w1, w2)


def _even_layer(x, mod, layer_idx, p, diff_tiles):
    e = layer_idx // 2
    w_in = p["even_w_in"][e]
    o1 = 2 * A_WIDTH
    o2 = o1 + 2 * A_WIDTH
    o3 = o2 + N_GATES
    w = {"qk": w_in[:, :o1].astype(BF16), "vo": w_in[:, o1:o2].astype(BF16),
         "g": w_in[:, o2:o3].astype(BF16), "b": w_in[:, o3:].astype(BF16)}
    qa, ka, va, oa, gates, qb, kb, vb = _even_in_proj(
        x, mod, p["mix_pre_g"][layer_idx][None], w, p["even_conv_w"][e],
        p["even_conv_b"][e][None], p["mlstm_gate_b"][e].reshape(1, N_GATES))
    gates_t = jnp.swapaxes(gates, 1, 2)
    h_fwd = _mlstm(qa, ka, va, gates, gates_t, reverse=False)
    mix_a = _mlstm(qa, ka, va, gates, gates_t, reverse=True, h_fwd=h_fwd, o_gate=oa,
                   norm_g=p["mlstm_norm_g"][e][None])
    mix_b = _diff_attention(qb, kb, vb, p["diff_lambda"][e], p["diff_norm_g"][e][None], diff_tiles,
                            layer_idx)
    return [mix_a, mix_b], p["even_w_out"][e].astype(BF16)


def _odd_layer(x, mod, layer_idx, p, c_tiles, d_tiles):
    o = layer_idx // 2
    bsz, s, _ = x.shape
    w_in = p["odd_w_in"][o]
    cw = C_WIDTH
    wq, wk, wv = w_in[:, :cw] * QK_SCALE, w_in[:, cw:2 * cw], w_in[:, 2 * cw:3 * cw]
    wc = jnp.stack([jnp.concatenate([t[:, g * C_GW:(g + 1) * C_GW] for t in (wq, wk, wv)], axis=1)
                    for g in range(len(C_PAIRS))]).astype(BF16)
    d0 = 3 * cw
    wqd = w_in[:, d0:d0 + D_WIDTH] * QK_SCALE
    wkd = w_in[:, d0 + D_WIDTH:d0 + D_WIDTH + D_KV_HEADS * HEAD_DIM]
    wvd = w_in[:, d0 + D_WIDTH + D_KV_HEADS * HEAD_DIM:]
    dup = lambda t: jnp.concatenate([t[:, g * HEAD_DIM:(g + 1) * HEAD_DIM]
                                     for g in range(D_KV_HEADS) for _ in range(2)], axis=1)
    wd = jnp.concatenate([wqd, dup(wkd), dup(wvd)], axis=1).astype(BF16)
    c0, c1, c2, qd, kd, vd = _odd_in_proj(x, mod, p["mix_pre_g"][layer_idx][None], {"c": wc, "d": wd})
    mix = []
    for g, (qkv, (window, dil)) in enumerate(zip((c0, c1, c2), C_PAIRS)):
        sub = s // dil
        strided = qkv.reshape(bsz, sub, dil * 3 * C_GW)
        o_g, lse_g = _local_attention(
            strided, strided, strided, c_tiles[g], seq=sub, n_res=dil, q_width=C_GW,
            q_col=lambda r: 3 * r, k_col=lambda r: 3 * r + 1, v_col=lambda r: 3 * r + 2,
            hw=window // (2 * dil), k_blocks=(0, 1), out_dtype=F32, with_lse=True)
        mix += [o_g.reshape(bsz, s, C_GW), lse_g.reshape(bsz, s, C_GW)]
    o_d = _local_attention(qd, kd, vd, d_tiles, seq=s, n_res=1, q_width=D_WIDTH,
                           q_col=lambda r: 0, k_col=lambda r: 0, v_col=lambda r: 0,
                           hw=D_HALF_WINDOW, k_blocks=(0, 0, 1, 1), out_dtype=BF16,
                           sink=p["sink_logit"][o])[0]
    mix.append(o_d)
    return mix, p["odd_w_out"][o].astype(BF16)


def _local_bias_tiles(table, seq, *, n_heads, col0, hw, dil):
    ti = min(LOCAL_TILE, LOCAL_OUTER, seq)
    return _bias_tiles(table, n_heads=n_heads, n_tiles=1, tq=ti, tk=ti + 2 * hw, col0=col0,
                       offset0=-hw, stride=0, dil=dil, band=hw)[:, 0]


def _trunk(x, mods, p):
    bsz, s, _ = x.shape
    table = p["rel_bias_table"]
    diff_tiles = _diff_bias_tiles(table, s)
    c_tiles = [_local_bias_tiles(table, s // dil, n_heads=C_HPG, col0=C_COL0 + g * C_HPG,
                                 hw=window // (2 * dil), dil=dil)
               for g, (window, dil) in enumerate(C_PAIRS)]
    d_tiles = _local_bias_tiles(table, s, n_heads=D_HEADS, col0=D_COL0, hw=D_HALF_WINDOW, dil=1)
    for l in range(DEPTH):
        mod = mods[l].reshape(bsz, 6, D_MODEL)
        if l % 2 == 0:
            mix, wo = _even_layer(x, mod, l, p, diff_tiles)
        else:
            mix, wo = _odd_layer(x, mod, l, p, c_tiles, d_tiles)
        gains = jnp.stack([p["mix_post_g"][l], p["mlp_pre_g"][l], p["mlp_post_g"][l]])
        x = _post_mixer(mix, x, mod, gains, wo, p["mlp_w1"][l].astype(BF16),
                        p["mlp_w2"][l].astype(BF16), even=l % 2 == 0)
    return x


def kernel(x_prompt, x_sample, c_prompt, c_sample, rel_bias_table, ada_w, ada_b, mix_pre_g, mix_post_g, mlp_pre_g, mlp_post_g, mlp_w1, mlp_w2, even_w_in, even_conv_w, even_conv_b, mlstm_gate_b, mlstm_norm_g, diff_lambda, diff_norm_g, even_w_out, odd_w_in, odd_w_out, sink_logit):
    p = dict(rel_bias_table=rel_bias_table, mix_pre_g=mix_pre_g, mix_post_g=mix_post_g,
             mlp_pre_g=mlp_pre_g, mlp_post_g=mlp_post_g, mlp_w1=mlp_w1, mlp_w2=mlp_w2,
             even_w_in=even_w_in, even_conv_w=even_conv_w, even_conv_b=even_conv_b,
             mlstm_gate_b=mlstm_gate_b, mlstm_norm_g=mlstm_norm_g, diff_lambda=diff_lambda,
             diff_norm_g=diff_norm_g, even_w_out=even_w_out, odd_w_in=odd_w_in,
             odd_w_out=odd_w_out, sink_logit=sink_logit)
    n_prompt = c_prompt.shape[0]
    mods = _modulation(jnp.concatenate([c_prompt, c_sample], axis=0), ada_w, ada_b)
    y_prompt = _trunk(x_prompt, mods[:, :n_prompt], p)
    y_sample = _trunk(x_sample, mods[:, n_prompt:], p)
    return (y_prompt, y_sample)
```

```python
import functools
import math

import jax
import jax.numpy as jnp
from jax import lax
from jax.experimental import pallas as pl
from jax.experimental.pallas import tpu as pltpu

F32 = jnp.float32
BF16 = jnp.bfloat16

D_MODEL = 1024
DEPTH = 4
HEAD_DIM = 64
PAIR = 2 * HEAD_DIM
EPS = 1e-6
NEG = -1e30
QK_SCALE = HEAD_DIM ** -0.5
LOG2E = math.log2(math.e)
A_HEADS = D_MODEL // 128
A_WIDTH = A_HEADS * HEAD_DIM
A_CONV = 5
A_CHUNK = 128
A_CHUNKS_PER_STEP = 4
N_GATES = 4 * A_HEADS
B_HEADS = D_MODEL // 256
B_WIDTH = B_HEADS * 2 * HEAD_DIM
C_PAIRS = ((128, 1), (512, 4), (2048, 16))
C_HPG = D_MODEL // 256
C_GW = C_HPG * HEAD_DIM
C_WIDTH = len(C_PAIRS) * C_GW
D_HEADS = D_MODEL // 128
D_KV_HEADS = D_HEADS // 4
D_WIDTH = D_HEADS * HEAD_DIM
D_HALF_WINDOW = 128
D_FF = 4 * D_MODEL
N_BUCKETS = 32
MAX_DISTANCE = 1024
B_COL0 = 0
C_COL0 = B_HEADS
D_COL0 = B_HEADS + len(C_PAIRS) * C_HPG

VMEM_LIMIT = 56 * 1024 * 1024
CONV_HALO = 16
TOKEN_TILE = 512
DIFF_TQ = 512
DIFF_TK = 512
DIFF_ROWS = 32
LOCAL_TILE = 128
LOCAL_OUTER = 512


def _cparams(*sem):
    return pltpu.CompilerParams(dimension_semantics=sem, vmem_limit_bytes=VMEM_LIMIT)


def _dot(a, b):
    return jnp.dot(a, b, preferred_element_type=F32)


def _dot_nt(a, b):
    return lax.dot_general(a, b, (((1,), (1,)), ((), ())), preferred_element_type=F32)


def _dot_tn(a, b):
    return lax.dot_general(a, b, (((0,), (0,)), ((), ())), preferred_element_type=F32)


def _rms(x, g):
    return x * lax.rsqrt(jnp.mean(x * x, axis=-1, keepdims=True) + EPS) * g


def _const_spec(shape):
    zeros = (0,) * len(shape)
    return pl.BlockSpec(shape, lambda *_: zeros)


def _rel_bucket(rel):
    half = N_BUCKETS // 2
    max_exact = half // 2
    ret = jnp.where(rel > 0, half, 0)
    n = jnp.abs(rel)
    nf = jnp.maximum(n, 1).astype(F32)
    large = max_exact + (jnp.log(nf / max_exact) / math.log(MAX_DISTANCE / max_exact)
                         * (half - max_exact)).astype(jnp.int32)
    large = jnp.minimum(large, half - 1)
    return ret + jnp.where(n < max_exact, n, large)


def _bias_tile_kernel(table_ref, out_ref, *, col0, offset0, stride, dil, band, scale):
    h = pl.program_id(0)
    t = pl.program_id(1)
    tq, tk = out_ref.shape[2:]
    qi = lax.broadcasted_iota(jnp.int32, (tq, tk), 0)
    ki = lax.broadcasted_iota(jnp.int32, (tq, tk), 1)
    delta = offset0 + t * stride + ki - qi
    bucket = _rel_bucket(delta * dil)
    val = jnp.zeros((tq, tk), F32)
    for j in range(N_BUCKETS):
        val = jnp.where(bucket == j, table_ref[j, col0 + h], val)
    if scale != 1.0:
        val = val * scale
    if band is not None:
        val = jnp.where(jnp.abs(delta) <= band, val, NEG)
    out_ref[0, 0] = val


def _bias_tiles(table, *, n_heads, n_tiles, tq, tk, col0, offset0, stride, dil, band, scale=1.0):
    return pl.pallas_call(
        functools.partial(_bias_tile_kernel, col0=col0, offset0=offset0, stride=stride, dil=dil,
                          band=band, scale=scale),
        grid=(n_heads, n_tiles),
        in_specs=[pl.BlockSpec(memory_space=pltpu.SMEM)],
        out_specs=pl.BlockSpec((1, 1, tq, tk), lambda h, t: (h, t, 0, 0)),
        out_shape=jax.ShapeDtypeStruct((n_heads, n_tiles, tq, tk), F32),
        compiler_params=_cparams("parallel", "parallel"),
        name="rel_bias_tiles",
    )(table)


def _mod_kernel(c_ref, w_ref, b_ref, o_ref):
    c = c_ref[...]
    a = (c * jax.nn.sigmoid(c)).astype(BF16)
    o_ref[0] = _dot(a, w_ref[0].astype(BF16)) + b_ref[0]


def _modulation(c_all, ada_w, ada_b):
    n, d = c_all.shape
    depth, _, width = ada_w.shape
    tn = 1536
    return pl.pallas_call(
        _mod_kernel,
        grid=(depth, width // tn),
        in_specs=[_const_spec((n, d)),
                  pl.BlockSpec((1, d, tn), lambda l, j: (l, 0, j)),
                  pl.BlockSpec((1, 1, tn), lambda l, j: (l, 0, j))],
        out_specs=pl.BlockSpec((1, n, tn), lambda l, j: (l, 0, j)),
        out_shape=jax.ShapeDtypeStruct((depth, n, width), F32),
        compiler_params=_cparams("parallel", "parallel"),
        name="adaln_modulation",
    )(c_all, ada_w, ada_b.reshape(depth, 1, width))


def _prenorm(x, g, shift, scale):
    return (_rms(x, g) * (1.0 + scale) + shift).astype(BF16)


def _even_in_kernel(x_ref, xp_ref, xn_ref, mod_ref, g_ref, wqk_ref, wvo_ref, wg_ref, wb_ref,
                    cw_ref, cb_ref, gb_ref,
                    qa_ref, ka_ref, va_ref, oa_ref, gates_ref, qb_ref, kb_ref, vb_ref, ext_ref):
    i = pl.program_id(1)
    n = pl.num_programs(1)
    tm = x_ref.shape[1]
    g = g_ref[...]
    shift = mod_ref[0, 0:1, :]
    scale = mod_ref[0, 1:2, :]
    h = _prenorm(x_ref[0], g, shift, scale)
    wqk = wqk_ref[...]
    prev_ok = (i > 0).astype(F32)
    next_ok = (i < n - 1).astype(F32)
    ext_ref[0:CONV_HALO, :] = _dot(_prenorm(xp_ref[0], g, shift, scale), wqk) * prev_ok
    ext_ref[CONV_HALO:CONV_HALO + tm, :] = _dot(h, wqk)
    ext_ref[CONV_HALO + tm:, :] = _dot(_prenorm(xn_ref[0], g, shift, scale), wqk) * next_ok
    acc = jnp.broadcast_to(cb_ref[...], (tm, 2 * A_WIDTH))
    for j in range(A_CONV):
        start = CONV_HALO - A_CONV // 2 + j
        acc = acc + cw_ref[j:j + 1, :] * ext_ref[start:start + tm, :]
    qk = acc * jax.nn.sigmoid(acc)
    qa_ref[0] = qk[:, :A_WIDTH].astype(BF16)
    ka_ref[0] = (qk[:, A_WIDTH:] * QK_SCALE).astype(BF16)
    vo = _dot(h, wvo_ref[...])
    va_ref[0] = vo[:, :A_WIDTH].astype(BF16)
    oa_ref[0] = vo[:, A_WIDTH:].astype(BF16)
    gates_ref[0] = _dot(h, wg_ref[...]) + gb_ref[...]
    qkv = _dot(h, wb_ref[...])
    qb_ref[0] = (qkv[:, :B_WIDTH] * (QK_SCALE * LOG2E)).astype(BF16)
    kb_ref[0] = qkv[:, B_WIDTH:2 * B_WIDTH].astype(BF16)
    vb_ref[0] = qkv[:, 2 * B_WIDTH:].astype(BF16)


def _even_in_proj(x, mod, g, w, conv_w, conv_b, gate_b):
    bsz, s, d = x.shape
    tm = min(TOKEN_TILE, s)
    nt = s // tm
    hb = tm // CONV_HALO
    n_halo = s // CONV_HALO
    tok = lambda width: pl.BlockSpec((1, tm, width), lambda b, i: (b, i, 0))
    outs = [A_WIDTH, A_WIDTH, A_WIDTH, A_WIDTH, N_GATES, B_WIDTH, B_WIDTH, B_WIDTH]
    dts = [BF16, BF16, BF16, BF16, F32, BF16, BF16, BF16]
    return pl.pallas_call(
        _even_in_kernel,
        grid=(bsz, nt),
        in_specs=[tok(d),
                  pl.BlockSpec((1, CONV_HALO, d), lambda b, i: (b, jnp.maximum(i * hb - 1, 0), 0)),
                  pl.BlockSpec((1, CONV_HALO, d),
                               lambda b, i: (b, jnp.minimum((i + 1) * hb, n_halo - 1), 0)),
                  pl.BlockSpec((1, 6, d), lambda b, i: (b, 0, 0)),
                  _const_spec((1, d)),
                  _const_spec(w["qk"].shape), _const_spec(w["vo"].shape),
                  _const_spec(w["g"].shape), _const_spec(w["b"].shape),
                  _const_spec(conv_w.shape), _const_spec(conv_b.shape), _const_spec(gate_b.shape)],
        out_specs=[tok(width) for width in outs],
        out_shape=[jax.ShapeDtypeStruct((bsz, s, width), dt) for width, dt in zip(outs, dts)],
        scratch_shapes=[pltpu.VMEM((tm + 2 * CONV_HALO, 2 * A_WIDTH), F32)],
        compiler_params=_cparams("parallel", "parallel"),
        name="even_in_proj",
    )(x, x, x, mod, g, w["qk"], w["vo"], w["g"], w["b"], conv_w, conv_b, gate_b)


def _odd_in_kernel(x_ref, mod_ref, g_ref, wc_ref, wd_ref, c0_ref, c1_ref, c2_ref, qd_ref, kd_ref, vd_ref):
    h = _prenorm(x_ref[0], g_ref[...], mod_ref[0, 0:1, :], mod_ref[0, 1:2, :])
    for g, ref in enumerate((c0_ref, c1_ref, c2_ref)):
        ref[0] = _dot(h, wc_ref[g]).astype(BF16)
    d = _dot(h, wd_ref[...])
    qd_ref[0] = d[:, :D_WIDTH].astype(BF16)
    kd_ref[0] = d[:, D_WIDTH:D_WIDTH + 2 * PAIR].astype(BF16)
    vd_ref[0] = d[:, D_WIDTH + 2 * PAIR:].astype(BF16)


def _odd_in_proj(x, mod, g, w):
    bsz, s, d = x.shape
    tm = min(TOKEN_TILE, s)
    tok = lambda width: pl.BlockSpec((1, tm, width), lambda b, i: (b, i, 0))
    outs = [3 * C_GW] * 3 + [D_WIDTH, 2 * PAIR, 2 * PAIR]
    return pl.pallas_call(
        _odd_in_kernel,
        grid=(bsz, s // tm),
        in_specs=[tok(d), pl.BlockSpec((1, 6, d), lambda b, i: (b, 0, 0)), _const_spec((1, d)),
                  _const_spec(w["c"].shape), _const_spec(w["d"].shape)],
        out_specs=[tok(width) for width in outs],
        out_shape=[jax.ShapeDtypeStruct((bsz, s, width), BF16) for width in outs],
        compiler_params=_cparams("parallel", "parallel"),
        name="odd_in_proj",
    )(x, mod, g, w["c"], w["d"])


def _log_sigmoid(x):
    return jnp.minimum(x, 0.0) - jnp.log1p(jnp.exp(-jnp.abs(x)))


def _mlstm_kernel(*refs, reverse, n_chunks):
    if reverse:
        (q_ref, kt_ref, v_ref, gc_ref, gr_ref, hf_ref, oa_ref, ng_ref, out_ref, c_scr, m_scr) = refs
    else:
        (q_ref, kt_ref, v_ref, gc_ref, gr_ref, out_ref, c_scr, m_scr) = refs
    L = A_CHUNK

    @pl.when(pl.program_id(1) == 0)
    def _():
        c_scr[...] = jnp.zeros_like(c_scr)
        m_scr[...] = jnp.zeros_like(m_scr)

    row = lax.broadcasted_iota(jnp.int32, (L, L), 0)
    col = lax.broadcasted_iota(jnp.int32, (L, L), 1)
    causal = (col >= row) if reverse else (col <= row)
    causal_t = (row >= col) if reverse else (row <= col)
    causal_f = causal.astype(F32)
    causal_tf = causal_t.astype(F32)
    logf_c = _log_sigmoid(gc_ref[0]) * LOG2E
    logf_r = _log_sigmoid(gr_ref[0]) * LOG2E
    i_col0 = 2 * A_HEADS if reverse else 0
    f_col0 = i_col0 + A_HEADS
    last = 0 if reverse else L - 1
    low = lax.broadcasted_iota(jnp.int32, (L, PAIR), 1) < HEAD_DIM
    one = jnp.ones((L, PAIR), BF16)

    for c in (range(n_chunks - 1, -1, -1) if reverse else range(n_chunks)):
        rows = slice(c * L, (c + 1) * L)
        cum_c = jnp.dot(causal_f, logf_c[rows], precision=lax.Precision.HIGHEST,
                        preferred_element_type=F32)
        cum_r = jnp.dot(logf_r[:, rows], causal_tf, precision=lax.Precision.HIGHEST,
                        preferred_element_type=F32)
        r_all = (gr_ref[0, i_col0:i_col0 + A_HEADS, rows] * LOG2E
                 - cum_r[f_col0:f_col0 + A_HEADS, :])
        r_max_all = jnp.max(r_all, axis=1, keepdims=True)
        b_tot_all = cum_r[f_col0:f_col0 + A_HEADS, last:last + 1]
        m_prev_rows = m_scr[...]
        m_prev_all = m_prev_rows[:, 0:1]
        m_new_all = b_tot_all + jnp.maximum(m_prev_all, r_max_all)
        m_scr[...] = jnp.broadcast_to(m_new_all, m_scr.shape)
        decay_all = jnp.exp2(b_tot_all + m_prev_all - m_new_all)
        gain_all = jnp.exp2(b_tot_all + r_max_all - m_new_all)
        rp_all = jnp.maximum(r_all, m_prev_rows)
        for p in range(A_HEADS // 2):
            cols = slice(p * PAIR, (p + 1) * PAIR)
            q2 = q_ref[0, rows, cols]
            v2 = v_ref[0, rows, cols]
            kt2 = kt_ref[0, cols, rows]
            kt2f = kt2.astype(F32)
            tots, floors = [], []
            for half in range(2):
                hd = 2 * p + half
                mine = low if half == 0 else jnp.logical_not(low)
                qm = jnp.where(mine, q2, jnp.zeros_like(q2))
                vx = jnp.where(mine, v2, one)
                r_r = r_all[hd:hd + 1, :]
                b_c = cum_c[:, f_col0 + hd:f_col0 + hd + 1]
                c_ext = c_scr[hd]
                a = jnp.max(jnp.where(causal, rp_all[hd:hd + 1, :], -jnp.inf), axis=1, keepdims=True)
                sc = _dot(qm, kt2) * jnp.exp2(jnp.where(causal, r_r, -jnp.inf) - a)
                tots.append(_dot(sc.astype(BF16), vx)
                            + jnp.exp2(m_prev_rows[hd:hd + 1, :] - a) * _dot(qm, c_ext.astype(BF16)))
                floors.append(jnp.exp2(-(b_c + a)))
                upd = _dot((kt2f * jnp.exp2(r_r - r_max_all[hd:hd + 1, :])).astype(BF16), vx)
                c_scr[hd] = decay_all[hd:hd + 1, :] * c_ext + gain_all[hd:hd + 1, :] * upd
            den = pltpu.roll(jnp.where(low, tots[1], tots[0]), HEAD_DIM, axis=1)
            floor = jnp.where(low, floors[0], floors[1])
            h2 = jnp.where(low, tots[0], tots[1]) * (1.0 / jnp.maximum(jnp.abs(den), floor))
            if not reverse:
                out_ref[0, rows, cols] = h2
            else:
                hs = h2 + hf_ref[0, rows, cols]
                sq = hs * hs
                ms0 = jnp.sum(jnp.where(low, sq, 0.0), axis=1, keepdims=True) / HEAD_DIM
                ms1 = jnp.sum(jnp.where(low, 0.0, sq), axis=1, keepdims=True) / HEAD_DIM
                rs = jnp.where(low, lax.rsqrt(ms0 + EPS), lax.rsqrt(ms1 + EPS))
                y = hs * rs * ng_ref[:, cols]
                out_ref[0, rows, cols] = (jax.nn.sigmoid(oa_ref[0, rows, cols].astype(F32)) * y
                                          ).astype(BF16)


def _mlstm(q, kt, v, gates, gates_t, *, reverse, h_fwd=None, o_gate=None, norm_g=None):
    bsz, s, _ = q.shape
    n_chunks = min(A_CHUNKS_PER_STEP, s // A_CHUNK)
    tl = n_chunks * A_CHUNK
    nb = s // tl
    pos = (lambda c: nb - 1 - c) if reverse else (lambda c: c)
    tok = lambda width: pl.BlockSpec((1, tl, width), lambda b, c: (b, pos(c), 0))
    tok_t = lambda width: pl.BlockSpec((1, width, tl), lambda b, c: (b, 0, pos(c)))
    in_specs = [tok(A_WIDTH), tok_t(A_WIDTH), tok(A_WIDTH), tok(N_GATES), tok_t(N_GATES)]
    args = [q, kt, v, gates, gates_t]
    if reverse:
        in_specs += [tok(A_WIDTH), tok(A_WIDTH), _const_spec((1, A_WIDTH))]
        args += [h_fwd, o_gate, norm_g]
    return pl.pallas_call(
        functools.partial(_mlstm_kernel, reverse=reverse, n_chunks=n_chunks),
        grid=(bsz, nb),
        in_specs=in_specs,
        out_specs=tok(A_WIDTH),
        out_shape=jax.ShapeDtypeStruct((bsz, s, A_WIDTH), BF16 if reverse else F32),
        scratch_shapes=[pltpu.VMEM((A_HEADS, PAIR, PAIR), F32), pltpu.VMEM((A_HEADS, 128), F32)],
        compiler_params=_cparams("parallel", "arbitrary"),
        name="mlstm_bwd" if reverse else "mlstm_fwd",
    )(*args)


def _diff_tile_range(tq, tk):
    unit = math.gcd(tq, tk)
    e_hi = pl.cdiv(MAX_DISTANCE + tq - 1, unit)
    e_lo = -pl.cdiv(MAX_DISTANCE + tk - 1, unit)
    return unit, e_lo, e_hi


def _diff_kernel(lam_ref, q_ref, k_ref, v_ref, bias_ref, g_ref, o_ref,
                 s0, s1, p0, p1, a0, a1, m_scr, acc_scr, *, tk, unit, e_lo, e_hi, lam_init):
    tq = q_ref.shape[1]
    rows = 2 * tq
    nk = k_ref.shape[1] // tk
    q0 = pl.program_id(2) * tq
    lane = lax.broadcasted_iota(jnp.int32, (tq, PAIR), 1)
    q2 = q_ref[0]
    zero = jnp.zeros_like(q2)
    qq = jnp.concatenate([jnp.where(lane < HEAD_DIM, q2, zero),
                          jnp.where(lane < HEAD_DIM, zero, q2)], axis=0)
    ones_blk = jnp.where(lax.broadcasted_iota(jnp.int32, (tk, PAIR), 1) == 0, 1.0, 0.0).astype(BF16)
    m_scr[...] = jnp.full(m_scr.shape, -jnp.inf, F32)
    acc_scr[...] = jnp.zeros(acc_scr.shape, F32)

    def scores(kb, s_ref):
        k0 = pl.multiple_of(kb * tk, tk)
        s_ref[...] = _dot_nt(qq, k_ref[0, pl.ds(k0, tk), :])

    def softmax(kb, s_ref, p_ref, a_ref):
        e = jnp.clip((kb * tk - q0) // unit, e_lo, e_hi) - e_lo
        for r in range(rows // DIFF_ROWS):
            rs = slice(r * DIFF_ROWS, (r + 1) * DIFF_ROWS)
            b0 = (r * DIFF_ROWS) % tq
            sc = s_ref[rs, :] + bias_ref[0, e, b0:b0 + DIFF_ROWS, :]
            m_old = m_scr[rs, :]
            m_new = jnp.maximum(m_old, jnp.max(sc, axis=1, keepdims=True))
            a_ref[rs, :] = jnp.exp2(m_old - m_new)
            m_scr[rs, :] = m_new
            p_ref[rs, :] = jnp.exp2(sc - m_new).astype(BF16)

    def accumulate(kb, p_ref, a_ref):
        k0 = pl.multiple_of(kb * tk, tk)
        v_ext = jnp.concatenate([v_ref[0, pl.ds(k0, tk), :], ones_blk], axis=1)
        acc_scr[...] = a_ref[...] * acc_scr[...] + _dot(p_ref[...], v_ext)

    scores(0, s0)
    scores(1, s1)
    softmax(0, s0, p0, a0)

    def steady(j, carry):
        t = 2 * j + 1
        scores(t + 1, s0)
        softmax(t, s1, p1, a1)
        accumulate(t - 1, p0, a0)
        scores(t + 2, s1)
        softmax(t + 1, s0, p0, a0)
        accumulate(t, p1, a1)
        return carry

    lax.fori_loop(0, (nk - 2) // 2, steady, 0)
    softmax(nk - 1, s1, p1, a1)
    accumulate(nk - 2, p0, a0)
    accumulate(nk - 1, p1, a1)

    acc = acc_scr[...]
    o = acc[:, :PAIR] * (1.0 / acc[:, PAIR:PAIR + 1])
    lam = lam_ref[...]
    lam_full = (jnp.exp(jnp.sum(lam[0:1] * lam[1:2], axis=1, keepdims=True))
                - jnp.exp(jnp.sum(lam[2:3] * lam[3:4], axis=1, keepdims=True)) + lam_init)
    o = o[:tq] - lam_full * o[tq:]
    o_ref[0] = (_rms(o, g_ref[...]) * (1.0 - lam_init)).astype(BF16)


def _diff_attention(q, k, v, lam, g, bias_tiles, layer_idx):
    bsz, s, _ = q.shape
    tq, tk = min(DIFF_TQ, s), min(DIFF_TK, s)
    assert (s // tk) % 2 == 0, "the key-block pipeline is unrolled by two"
    unit, e_lo, e_hi = _diff_tile_range(tq, tk)
    n_tiles = e_hi - e_lo + 1
    lam_init = 0.8 - 0.6 * math.exp(-0.3 * layer_idx)
    return pl.pallas_call(
        functools.partial(_diff_kernel, tk=tk, unit=unit, e_lo=e_lo, e_hi=e_hi, lam_init=lam_init),
        grid=(bsz, B_HEADS, s // tq),
        in_specs=[_const_spec(lam.shape),
                  pl.BlockSpec((1, tq, PAIR), lambda b, h, i: (b, i, h)),
                  pl.BlockSpec((1, s, PAIR), lambda b, h, i: (b, 0, h)),
                  pl.BlockSpec((1, s, PAIR), lambda b, h, i: (b, 0, h)),
                  pl.BlockSpec((1, n_tiles, tq, tk), lambda b, h, i: (h, 0, 0, 0)),
                  pl.BlockSpec((1, PAIR), lambda b, h, i: (0, h))],
        out_specs=pl.BlockSpec((1, tq, PAIR), lambda b, h, i: (b, i, h)),
        out_shape=jax.ShapeDtypeStruct((bsz, s, B_WIDTH), BF16),
        scratch_shapes=[pltpu.VMEM((2 * tq, tk), F32), pltpu.VMEM((2 * tq, tk), F32),
                        pltpu.VMEM((2 * tq, tk), BF16), pltpu.VMEM((2 * tq, tk), BF16),
                        pltpu.VMEM((2 * tq, 1), F32), pltpu.VMEM((2 * tq, 1), F32),
                        pltpu.VMEM((2 * tq, 1), F32), pltpu.VMEM((2 * tq, 2 * PAIR), F32)],
        compiler_params=_cparams("parallel", "parallel", "arbitrary"),
        name="diff_attention",
    )(lam, q, k, v, bias_tiles, g)


def _diff_bias_tiles(table, s):
    tq, tk = min(DIFF_TQ, s), min(DIFF_TK, s)
    unit, e_lo, e_hi = _diff_tile_range(tq, tk)
    return _bias_tiles(table, n_heads=B_HEADS, n_tiles=e_hi - e_lo + 1, tq=tq, tk=tk, col0=B_COL0,
                       offset0=e_lo * unit, stride=unit, dil=1, band=None, scale=LOG2E)


def _local_kernel(*refs, hw, ti, k_blocks, with_sink, with_lse):
    refs = list(refs)
    sink_ref = refs.pop(0) if with_sink else None
    (q_ref, kp_ref, kc_ref, kn_ref, vp_ref, vc_ref, vn_ref, bias_ref) = refs[:8]
    o_ref = refs[8]
    lse_ref = refs[9] if with_lse else None
    kext, vext = refs[-2:]
    tb = q_ref.shape[1]
    i = pl.program_id(2)
    seq = pl.num_programs(2) * tb
    kext[0:hw, :] = kp_ref[0]
    kext[hw:hw + tb, :] = kc_ref[0]
    kext[hw + tb:, :] = kn_ref[0]
    vext[0:hw, :] = vp_ref[0]
    vext[hw:hw + tb, :] = vc_ref[0]
    vext[hw + tb:, :] = vn_ref[0]
    wk = ti + 2 * hw
    n_sub = tb // ti
    low = lax.broadcasted_iota(jnp.int32, (ti, PAIR), 1) < HEAD_DIM
    kj = lax.broadcasted_iota(jnp.int32, (1, wk), 1)
    for j in range(n_sub):
        rows = slice(j * ti, (j + 1) * ti)
        krows = slice(j * ti, j * ti + wk)
        edge = None
        if j in (0, n_sub - 1):
            kpos = i * tb + j * ti - hw + kj
            edge = jnp.where((kpos >= 0) & (kpos < seq), 0.0, NEG)
        for kb in sorted(set(k_blocks)):
            pairs = [p for p, b in enumerate(k_blocks) if b == kb]
            kk = kext[krows, kb * PAIR:(kb + 1) * PAIR]
            vv = vext[krows, kb * PAIR:(kb + 1) * PAIR]
            stack = []
            for p in pairs:
                q2 = q_ref[0, rows, p * PAIR:(p + 1) * PAIR]
                zero = jnp.zeros_like(q2)
                stack += [jnp.where(low, q2, zero), jnp.where(low, zero, q2)]
            sc_all = _dot_nt(jnp.concatenate(stack, axis=0), kk)
            probs, inv_dens, lses = [], [], []
            for n, p in enumerate(pairs):
                for half in range(2):
                    hd = 2 * p + half
                    sc = sc_all[(2 * n + half) * ti:(2 * n + half + 1) * ti] + bias_ref[hd]
                    if edge is not None:
                        sc = sc + edge
                    m = jnp.max(sc, axis=1, keepdims=True)
                    if with_sink:
                        m = jnp.maximum(m, sink_ref[hd])
                    pr = jnp.exp(sc - m)
                    den = jnp.sum(pr, axis=1, keepdims=True)
                    if with_sink:
                        den = den + jnp.exp(sink_ref[hd] - m)
                    probs.append(pr.astype(BF16))
                    inv_dens.append(1.0 / den)
                    lses.append(m + jnp.log(den))
            o_all = _dot(jnp.concatenate(probs, axis=0), vv)
            for n, p in enumerate(pairs):
                o0 = o_all[2 * n * ti:(2 * n + 1) * ti] * inv_dens[2 * n]
                o1 = o_all[(2 * n + 1) * ti:(2 * n + 2) * ti] * inv_dens[2 * n + 1]
                o_ref[0, rows, p * PAIR:(p + 1) * PAIR] = jnp.where(low, o0, o1).astype(o_ref.dtype)
                if with_lse:
                    lse_ref[0, rows, p * PAIR:(p + 1) * PAIR] = jnp.where(low, lses[2 * n],
                                                                           lses[2 * n + 1])


def _local_attention(q_arr, k_arr, v_arr, bias, *, seq, n_res, q_width, q_col, k_col, v_col, hw,
                     k_blocks, out_dtype, sink=None, with_lse=False):
    bsz = q_arr.shape[0]
    tb = min(LOCAL_OUTER, seq)
    ti = min(LOCAL_TILE, tb)
    kw = len(set(k_blocks)) * PAIR
    hb = tb // hw
    n_halo = seq // hw
    cur = lambda width, colf: pl.BlockSpec((1, tb, width), lambda b, r, i: (b, i, colf(r)))
    prev = lambda colf: pl.BlockSpec((1, hw, kw), lambda b, r, i: (b, jnp.maximum(i * hb - 1, 0), colf(r)))
    nxt = lambda colf: pl.BlockSpec(
        (1, hw, kw), lambda b, r, i: (b, jnp.minimum((i + 1) * hb, n_halo - 1), colf(r)))
    in_specs = [cur(q_width, q_col), prev(k_col), cur(kw, k_col), nxt(k_col),
                prev(v_col), cur(kw, v_col), nxt(v_col), _const_spec(bias.shape)]
    args = [q_arr, k_arr, k_arr, k_arr, v_arr, v_arr, v_arr, bias]
    if sink is not None:
        in_specs.insert(0, pl.BlockSpec(memory_space=pltpu.SMEM))
        args.insert(0, sink)
    out_spec = pl.BlockSpec((1, tb, q_width), lambda b, r, i: (b, i, r))
    out_sds = jax.ShapeDtypeStruct((bsz, seq, n_res * q_width), out_dtype)
    out_specs, out_shape = [out_spec], [out_sds]
    if with_lse:
        out_specs.append(out_spec)
        out_shape.append(jax.ShapeDtypeStruct((bsz, seq, n_res * q_width), F32))
    return pl.pallas_call(
        functools.partial(_local_kernel, hw=hw, ti=ti, k_blocks=tuple(k_blocks),
                          with_sink=sink is not None, with_lse=with_lse),
        grid=(bsz, n_res, seq // tb),
        in_specs=in_specs,
        out_specs=out_specs,
        out_shape=out_shape,
        scratch_shapes=[pltpu.VMEM((tb + 2 * hw, kw), BF16), pltpu.VMEM((tb + 2 * hw, kw), BF16)],
        compiler_params=_cparams("parallel", "parallel", "parallel"),
        name="local_attention",
    )(*args)


def _post_kernel(*refs, even):
    refs = list(refs)
    if even:
        a_ref, b_ref = refs[:2]
        rest = refs[2:]
    else:
        c_refs = refs[:6]
        d_ref = refs[6]
        rest = refs[7:]
    x_ref, mod_ref, gains_ref, wo_ref, w1_ref, w2_ref, out_ref = rest
    if even:
        y = _dot(a_ref[0], wo_ref[0:A_WIDTH, :]) + _dot(b_ref[0], wo_ref[A_WIDTH:, :])
    else:
        os_ = [c_refs[2 * g][0] for g in range(3)]
        ls_ = [c_refs[2 * g + 1][0] for g in range(3)]
        mx = jnp.maximum(jnp.maximum(ls_[0], ls_[1]), ls_[2])
        es = [jnp.exp(l - mx) for l in ls_]
        inv = 1.0 / (es[0] + es[1] + es[2])
        y = _dot(d_ref[0], wo_ref[C_WIDTH:, :])
        for g in range(3):
            y = y + _dot((os_[g] * (es[g] * inv)).astype(BF16), wo_ref[g * C_GW:(g + 1) * C_GW, :])
    gate1 = mod_ref[0, 2:3, :]
    shift2 = mod_ref[0, 3:4, :]
    scale2 = mod_ref[0, 4:5, :]
    gate2 = mod_ref[0, 5:6, :]
    x = x_ref[0] + gate1 * _rms(y, gains_ref[0:1, :])
    h = _prenorm(x, gains_ref[1:2, :], shift2, scale2)
    acc = jnp.zeros_like(x)
    step = D_MODEL
    for j in range(D_FF // step):
        u = jnp.maximum(_dot(h, w1_ref[:, j * step:(j + 1) * step]), 0.0)
        acc = acc + _dot((u * u).astype(BF16), w2_ref[j * step:(j + 1) * step, :])
    out_ref[0] = x + gate2 * _rms(acc, gains_ref[2:3, :])


def _post_mixer(mix, x, mod, gains, wo, w1, w2, *, even):
    bsz, s, d = x.shape
    tm = min(TOKEN_TILE, s)
    tok = lambda width: pl.BlockSpec((1, tm, width), lambda b, i: (b, i, 0))
    resident = lambda a: pl.BlockSpec(a.shape, lambda b, i: (0, 0), pipeline_mode=pl.Buffered(1))
    in_specs = [tok(m.shape[-1]) for m in mix]
    in_specs += [tok(d), pl.BlockSpec((1, 6, d), lambda b, i: (b, 0, 0)), _const_spec(gains.shape),
                 resident(wo), resident(w1), resident(w2)]
    return pl.pallas_call(
        functools.partial(_post_kernel, even=even),
        grid=(bsz, s // tm),
        in_specs=in_specs,
        out_specs=tok(d),
        out_shape=jax.ShapeDtypeStruct((bsz, s, d), F32),
        compiler_params=_cparams("parallel", "parallel"),
        name="post_mixer_mlp",
    )(*mix, x, mod, gains, wo, w1, w2)


def _even_layer(x, mod, layer_idx, p, diff_tiles):
    e = layer_idx // 2
    w_in = p["even_w_in"][e]
    o1 = 2 * A_WIDTH
    o2 = o1 + 2 * A_WIDTH
    o3 = o2 + N_GATES
    w = {"qk": w_in[:, :o1].astype(BF16), "vo": w_in[:, o1:o2].astype(BF16),
         "g": w_in[:, o2:o3].astype(BF16), "b": w_in[:, o3:].astype(BF16)}
    qa, ka, va, oa, gates, qb, kb, vb = _even_in_proj(
        x, mod, p["mix_pre_g"][layer_idx][None], w, p["even_conv_w"][e],
        p["even_conv_b"][e][None], p["mlstm_gate_b"][e].reshape(1, N_GATES))
    gates_t = jnp.swapaxes(gates, 1, 2)
    ka_t = jnp.swapaxes(ka, 1, 2)
    h_fwd = _mlstm(qa, ka_t, va, gates, gates_t, reverse=False)
    mix_a = _mlstm(qa, ka_t, va, gates, gates_t, reverse=True, h_fwd=h_fwd, o_gate=oa,
                   norm_g=p["mlstm_norm_g"][e][None])
    mix_b = _diff_attention(qb, kb, vb, p["diff_lambda"][e], p["diff_norm_g"][e][None], diff_tiles,
                            layer_idx)
    return [mix_a, mix_b], p["even_w_out"][e].astype(BF16)


def _odd_layer(x, mod, layer_idx, p, c_tiles, d_tiles):
    o = layer_idx // 2
    bsz, s, _ = x.shape
    w_in = p["odd_w_in"][o]
    cw = C_WIDTH
    wq, wk, wv = w_in[:, :cw] * QK_SCALE, w_in[:, cw:2 * cw], w_in[:, 2 * cw:3 * cw]
    wc = jnp.stack([jnp.concatenate([t[:, g * C_GW:(g + 1) * C_GW] for t in (wq, wk, wv)], axis=1)
                    for g in range(len(C_PAIRS))]).astype(BF16)
    d0 = 3 * cw
    wqd = w_in[:, d0:d0 + D_WIDTH] * QK_SCALE
    wkd = w_in[:, d0 + D_WIDTH:d0 + D_WIDTH + D_KV_HEADS * HEAD_DIM]
    wvd = w_in[:, d0 + D_WIDTH + D_KV_HEADS * HEAD_DIM:]
    dup = lambda t: jnp.concatenate([t[:, g * HEAD_DIM:(g + 1) * HEAD_DIM]
                                     for g in range(D_KV_HEADS) for _ in range(2)], axis=1)
    wd = jnp.concatenate([wqd, dup(wkd), dup(wvd)], axis=1).astype(BF16)
    c0, c1, c2, qd, kd, vd = _odd_in_proj(x, mod, p["mix_pre_g"][layer_idx][None], {"c": wc, "d": wd})
    mix = []
    for g, (qkv, (window, dil)) in enumerate(zip((c0, c1, c2), C_PAIRS)):
        sub = s // dil
        strided = qkv.reshape(bsz, sub, dil * 3 * C_GW)
        o_g, lse_g = _local_attention(
            strided, strided, strided, c_tiles[g], seq=sub, n_res=dil, q_width=C_GW,
            q_col=lambda r: 3 * r, k_col=lambda r: 3 * r + 1, v_col=lambda r: 3 * r + 2,
            hw=window // (2 * dil), k_blocks=(0, 1), out_dtype=F32, with_lse=True)
        mix += [o_g.reshape(bsz, s, C_GW), lse_g.reshape(bsz, s, C_GW)]
    o_d = _local_attention(qd, kd, vd, d_tiles, seq=s, n_res=1, q_width=D_WIDTH,
                           q_col=lambda r: 0, k_col=lambda r: 0, v_col=lambda r: 0,
                           hw=D_HALF_WINDOW, k_blocks=(0, 0, 1, 1), out_dtype=BF16,
                           sink=p["sink_logit"][o])[0]
    mix.append(o_d)
    return mix, p["odd_w_out"][o].astype(BF16)


def _local_bias_tiles(table, seq, *, n_heads, col0, hw, dil):
    ti = min(LOCAL_TILE, LOCAL_OUTER, seq)
    return _bias_tiles(table, n_heads=n_heads, n_tiles=1, tq=ti, tk=ti + 2 * hw, col0=col0,
                       offset0=-hw, stride=0, dil=dil, band=hw)[:, 0]


def _trunk(x, mods, p):
    bsz, s, _ = x.shape
    table = p["rel_bias_table"]
    diff_tiles = _diff_bias_tiles(table, s)
    c_tiles = [_local_bias_tiles(table, s // dil, n_heads=C_HPG, col0=C_COL0 + g * C_HPG,
                                 hw=window // (2 * dil), dil=dil)
               for g, (window, dil) in enumerate(C_PAIRS)]
    d_tiles = _local_bias_tiles(table, s, n_heads=D_HEADS, col0=D_COL0, hw=D_HALF_WINDOW, dil=1)
    for l in range(DEPTH):
        mod = mods[l].reshape(bsz, 6, D_MODEL)
        if l % 2 == 0:
            mix, wo = _even_layer(x, mod, l, p, diff_tiles)
        else:
            mix, wo = _odd_layer(x, mod, l, p, c_tiles, d_tiles)
        gains = jnp.stack([p["mix_post_g"][l], p["mlp_pre_g"][l], p["mlp_post_g"][l]])
        x = _post_mixer(mix, x, mod, gains, wo, p["mlp_w1"][l].astype(BF16),
                        p["mlp_w2"][l].astype(BF16), even=l % 2 == 0)
    return x


def kernel(x_prompt, x_sample, c_prompt, c_sample, rel_bias_table, ada_w, ada_b, mix_pre_g, mix_post_g, mlp_pre_g, mlp_post_g, mlp_w1, mlp_w2, even_w_in, even_conv_w, even_conv_b, mlstm_gate_b, mlstm_norm_g, diff_lambda, diff_norm_g, even_w_out, odd_w_in, odd_w_out, sink_logit):
    p = dict(rel_bias_table=rel_bias_table, mix_pre_g=mix_pre_g, mix_post_g=mix_post_g,
             mlp_pre_g=mlp_pre_g, mlp_post_g=mlp_post_g, mlp_w1=mlp_w1, mlp_w2=mlp_w2,
             even_w_in=even_w_in, even_conv_w=even_conv_w, even_conv_b=even_conv_b,
             mlstm_gate_b=mlstm_gate_b, mlstm_norm_g=mlstm_norm_g, diff_lambda=diff_lambda,
             diff_norm_g=diff_norm_g, even_w_out=even_w_out, odd_w_in=odd_w_in,
             odd_w_out=odd_w_out, sink_logit=sink_logit)
    n_prompt = c_prompt.shape[0]
    mods = _modulation(jnp.concatenate([c_prompt, c_sample], axis=0), ada_w, ada_b)
    y_prompt = _trunk(x_prompt, mods[:, :n_prompt], p)
    y_sample = _trunk(x_sample, mods[:, n_prompt:], p)
    return (y_prompt, y_sample)
```

```python
import functools
import math

import jax
import jax.numpy as jnp
from jax import lax
from jax.experimental import pallas as pl
from jax.experimental.pallas import tpu as pltpu

F32 = jnp.float32
BF16 = jnp.bfloat16

D_MODEL = 1024
DEPTH = 4
HEAD_DIM = 64
PAIR = 2 * HEAD_DIM
EPS = 1e-6
NEG = -1e30
QK_SCALE = HEAD_DIM ** -0.5
LOG2E = math.log2(math.e)
A_HEADS = D_MODEL // 128
A_WIDTH = A_HEADS * HEAD_DIM
A_CONV = 5
A_CHUNK = 128
A_CHUNKS_PER_STEP = 4
N_GATES = 4 * A_HEADS
B_HEADS = D_MODEL // 256
B_WIDTH = B_HEADS * 2 * HEAD_DIM
C_PAIRS = ((128, 1), (512, 4), (2048, 16))
C_HPG = D_MODEL // 256
C_GW = C_HPG * HEAD_DIM
C_WIDTH = len(C_PAIRS) * C_GW
D_HEADS = D_MODEL // 128
D_KV_HEADS = D_HEADS // 4
D_WIDTH = D_HEADS * HEAD_DIM
D_HALF_WINDOW = 128
D_FF = 4 * D_MODEL
N_BUCKETS = 32
MAX_DISTANCE = 1024
B_COL0 = 0
C_COL0 = B_HEADS
D_COL0 = B_HEADS + len(C_PAIRS) * C_HPG

VMEM_LIMIT = 56 * 1024 * 1024
CONV_HALO = 16
TOKEN_TILE = 512
DIFF_TQ = 512
DIFF_TK = 512
DIFF_ONES_ROWS = 16
LOCAL_TILE = 128
LOCAL_OUTER = 512


def _cparams(*sem):
    return pltpu.CompilerParams(dimension_semantics=sem, vmem_limit_bytes=VMEM_LIMIT)


def _dot(a, b):
    return jnp.dot(a, b, preferred_element_type=F32)


def _dot_nt(a, b):
    return lax.dot_general(a, b, (((1,), (1,)), ((), ())), preferred_element_type=F32)


def _dot_tn(a, b):
    return lax.dot_general(a, b, (((0,), (0,)), ((), ())), preferred_element_type=F32)


def _rms(x, g):
    return x * lax.rsqrt(jnp.mean(x * x, axis=-1, keepdims=True) + EPS) * g


def _const_spec(shape):
    zeros = (0,) * len(shape)
    return pl.BlockSpec(shape, lambda *_: zeros)


def _rel_bucket(rel):
    half = N_BUCKETS // 2
    max_exact = half // 2
    ret = jnp.where(rel > 0, half, 0)
    n = jnp.abs(rel)
    nf = jnp.maximum(n, 1).astype(F32)
    large = max_exact + (jnp.log(nf / max_exact) / math.log(MAX_DISTANCE / max_exact)
                         * (half - max_exact)).astype(jnp.int32)
    large = jnp.minimum(large, half - 1)
    return ret + jnp.where(n < max_exact, n, large)


def _bias_tile_kernel(table_ref, out_ref, *, col0, offset0, stride, dil, band, scale, keys_on_rows):
    h = pl.program_id(0)
    t = pl.program_id(1)
    shape = out_ref.shape[2:]
    qi = lax.broadcasted_iota(jnp.int32, shape, 1 if keys_on_rows else 0)
    ki = lax.broadcasted_iota(jnp.int32, shape, 0 if keys_on_rows else 1)
    delta = offset0 + t * stride + ki - qi
    bucket = _rel_bucket(delta * dil)
    val = jnp.zeros(shape, F32)
    for j in range(N_BUCKETS):
        val = jnp.where(bucket == j, table_ref[j, col0 + h], val)
    if scale != 1.0:
        val = val * scale
    if band is not None:
        val = jnp.where(jnp.abs(delta) <= band, val, NEG)
    out_ref[0, 0] = val


def _bias_tiles(table, *, n_heads, n_tiles, tq, tk, col0, offset0, stride, dil, band, scale=1.0,
                keys_on_rows=False):
    shape = (tk, tq) if keys_on_rows else (tq, tk)
    return pl.pallas_call(
        functools.partial(_bias_tile_kernel, col0=col0, offset0=offset0, stride=stride, dil=dil,
                          band=band, scale=scale, keys_on_rows=keys_on_rows),
        grid=(n_heads, n_tiles),
        in_specs=[pl.BlockSpec(memory_space=pltpu.SMEM)],
        out_specs=pl.BlockSpec((1, 1) + shape, lambda h, t: (h, t, 0, 0)),
        out_shape=jax.ShapeDtypeStruct((n_heads, n_tiles) + shape, F32),
        compiler_params=_cparams("parallel", "parallel"),
        name="rel_bias_tiles",
    )(table)


def _mod_kernel(c_ref, w_ref, b_ref, o_ref):
    c = c_ref[...]
    a = (c * jax.nn.sigmoid(c)).astype(BF16)
    o_ref[0] = _dot(a, w_ref[0].astype(BF16)) + b_ref[0]


def _modulation(c_all, ada_w, ada_b):
    n, d = c_all.shape
    depth, _, width = ada_w.shape
    tn = 1536
    return pl.pallas_call(
        _mod_kernel,
        grid=(depth, width // tn),
        in_specs=[_const_spec((n, d)),
                  pl.BlockSpec((1, d, tn), lambda l, j: (l, 0, j)),
                  pl.BlockSpec((1, 1, tn), lambda l, j: (l, 0, j))],
        out_specs=pl.BlockSpec((1, n, tn), lambda l, j: (l, 0, j)),
        out_shape=jax.ShapeDtypeStruct((depth, n, width), F32),
        compiler_params=_cparams("parallel", "parallel"),
        name="adaln_modulation",
    )(c_all, ada_w, ada_b.reshape(depth, 1, width))


def _prenorm(x, g, shift, scale):
    return (_rms(x, g) * (1.0 + scale) + shift).astype(BF16)


def _even_in_kernel(x_ref, xp_ref, xn_ref, mod_ref, g_ref, wqk_ref, wvo_ref, wg_ref, wb_ref,
                    cw_ref, cb_ref, gb_ref,
                    qa_ref, ka_ref, va_ref, oa_ref, gates_ref, qb_ref, kb_ref, vb_ref, hx_ref, ext_ref):
    i = pl.program_id(1)
    n = pl.num_programs(1)
    tm = x_ref.shape[1]
    g = g_ref[...]
    shift = mod_ref[0, 0:1, :]
    scale = mod_ref[0, 1:2, :]
    h = _prenorm(x_ref[0], g, shift, scale)
    hx_ref[0:CONV_HALO, :] = _prenorm(xp_ref[0], g, shift, scale)
    hx_ref[CONV_HALO:CONV_HALO + tm, :] = h
    hx_ref[CONV_HALO + tm:, :] = _prenorm(xn_ref[0], g, shift, scale)
    ext_ref[...] = _dot(hx_ref[...], wqk_ref[...])
    ext_ref[0:CONV_HALO, :] = ext_ref[0:CONV_HALO, :] * (i > 0).astype(F32)
    ext_ref[CONV_HALO + tm:, :] = ext_ref[CONV_HALO + tm:, :] * (i < n - 1).astype(F32)
    acc = jnp.broadcast_to(cb_ref[...], (tm, 2 * A_WIDTH))
    for j in range(A_CONV):
        start = CONV_HALO - A_CONV // 2 + j
        acc = acc + cw_ref[j:j + 1, :] * ext_ref[start:start + tm, :]
    qk = acc * jax.nn.sigmoid(acc)
    qa_ref[0] = qk[:, :A_WIDTH].astype(BF16)
    ka_ref[0] = (qk[:, A_WIDTH:] * QK_SCALE).astype(BF16)
    vo = _dot(h, wvo_ref[...])
    va_ref[0] = vo[:, :A_WIDTH].astype(BF16)
    oa_ref[0] = vo[:, A_WIDTH:].astype(BF16)
    gates_ref[0] = _dot(h, wg_ref[...]) + gb_ref[...]
    qkv = _dot(h, wb_ref[...])
    qb_ref[0] = (qkv[:, :B_WIDTH] * (QK_SCALE * LOG2E)).astype(BF16)
    kb_ref[0] = qkv[:, B_WIDTH:2 * B_WIDTH].astype(BF16)
    vb_ref[0] = qkv[:, 2 * B_WIDTH:].astype(BF16)


def _even_in_proj(x, mod, g, w, conv_w, conv_b, gate_b):
    bsz, s, d = x.shape
    tm = min(TOKEN_TILE, s)
    nt = s // tm
    hb = tm // CONV_HALO
    n_halo = s // CONV_HALO
    tok = lambda width: pl.BlockSpec((1, tm, width), lambda b, i: (b, i, 0))
    outs = [A_WIDTH, A_WIDTH, A_WIDTH, A_WIDTH, N_GATES, B_WIDTH, B_WIDTH, B_WIDTH]
    dts = [BF16, BF16, BF16, BF16, F32, BF16, BF16, BF16]
    return pl.pallas_call(
        _even_in_kernel,
        grid=(bsz, nt),
        in_specs=[tok(d),
                  pl.BlockSpec((1, CONV_HALO, d), lambda b, i: (b, jnp.maximum(i * hb - 1, 0), 0)),
                  pl.BlockSpec((1, CONV_HALO, d),
                               lambda b, i: (b, jnp.minimum((i + 1) * hb, n_halo - 1), 0)),
                  pl.BlockSpec((1, 6, d), lambda b, i: (b, 0, 0)),
                  _const_spec((1, d)),
                  _const_spec(w["qk"].shape), _const_spec(w["vo"].shape),
                  _const_spec(w["g"].shape), _const_spec(w["b"].shape),
                  _const_spec(conv_w.shape), _const_spec(conv_b.shape), _const_spec(gate_b.shape)],
        out_specs=[tok(width) for width in outs],
        out_shape=[jax.ShapeDtypeStruct((bsz, s, width), dt) for width, dt in zip(outs, dts)],
        scratch_shapes=[pltpu.VMEM((tm + 2 * CONV_HALO, d), BF16),
                        pltpu.VMEM((tm + 2 * CONV_HALO, 2 * A_WIDTH), F32)],
        compiler_params=_cparams("parallel", "parallel"),
        name="even_in_proj",
    )(x, x, x, mod, g, w["qk"], w["vo"], w["g"], w["b"], conv_w, conv_b, gate_b)


def _odd_in_kernel(x_ref, mod_ref, g_ref, wc_ref, wd_ref, c0_ref, c1_ref, c2_ref, qd_ref, kd_ref, vd_ref):
    h = _prenorm(x_ref[0], g_ref[...], mod_ref[0, 0:1, :], mod_ref[0, 1:2, :])
    for g, ref in enumerate((c0_ref, c1_ref, c2_ref)):
        ref[0] = _dot(h, wc_ref[g]).astype(BF16)
    d = _dot(h, wd_ref[...])
    qd_ref[0] = d[:, :D_WIDTH].astype(BF16)
    kd_ref[0] = d[:, D_WIDTH:D_WIDTH + 2 * PAIR].astype(BF16)
    vd_ref[0] = d[:, D_WIDTH + 2 * PAIR:].astype(BF16)


def _odd_in_proj(x, mod, g, w):
    bsz, s, d = x.shape
    tm = min(TOKEN_TILE, s)
    tok = lambda width: pl.BlockSpec((1, tm, width), lambda b, i: (b, i, 0))
    outs = [3 * C_GW] * 3 + [D_WIDTH, 2 * PAIR, 2 * PAIR]
    return pl.pallas_call(
        _odd_in_kernel,
        grid=(bsz, s // tm),
        in_specs=[tok(d), pl.BlockSpec((1, 6, d), lambda b, i: (b, 0, 0)), _const_spec((1, d)),
                  _const_spec(w["c"].shape), _const_spec(w["d"].shape)],
        out_specs=[tok(width) for width in outs],
        out_shape=[jax.ShapeDtypeStruct((bsz, s, width), BF16) for width in outs],
        compiler_params=_cparams("parallel", "parallel"),
        name="odd_in_proj",
    )(x, mod, g, w["c"], w["d"])


def _log_sigmoid(x):
    return jnp.minimum(x, 0.0) - jnp.log1p(jnp.exp(-jnp.abs(x)))


def _mlstm_kernel(*refs, reverse, n_chunks):
    if reverse:
        (q_ref, kt_ref, v_ref, gc_ref, gr_ref, hf_ref, oa_ref, ng_ref, out_ref, c_scr, m_scr) = refs
    else:
        (q_ref, kt_ref, v_ref, gc_ref, gr_ref, out_ref, c_scr, m_scr) = refs
    L = A_CHUNK

    @pl.when(pl.program_id(1) == 0)
    def _():
        c_scr[...] = jnp.zeros_like(c_scr)
        m_scr[...] = jnp.zeros_like(m_scr)

    row = lax.broadcasted_iota(jnp.int32, (L, L), 0)
    col = lax.broadcasted_iota(jnp.int32, (L, L), 1)
    causal = (col >= row) if reverse else (col <= row)
    causal_t = (row >= col) if reverse else (row <= col)
    causal_f = causal.astype(F32)
    causal_tf = causal_t.astype(F32)
    logf_c = _log_sigmoid(gc_ref[0]) * LOG2E
    logf_r = _log_sigmoid(gr_ref[0]) * LOG2E
    i_col0 = 2 * A_HEADS if reverse else 0
    f_col0 = i_col0 + A_HEADS
    last = 0 if reverse else L - 1
    low = lax.broadcasted_iota(jnp.int32, (L, PAIR), 1) < HEAD_DIM
    one = jnp.ones((L, PAIR), BF16)

    for c in (range(n_chunks - 1, -1, -1) if reverse else range(n_chunks)):
        rows = slice(c * L, (c + 1) * L)
        cum_c = jnp.dot(causal_f, logf_c[rows], precision=lax.Precision.HIGHEST,
                        preferred_element_type=F32)
        cum_r = jnp.dot(logf_r[:, rows], causal_tf, precision=lax.Precision.HIGHEST,
                        preferred_element_type=F32)
        r_all = (gr_ref[0, i_col0:i_col0 + A_HEADS, rows] * LOG2E
                 - cum_r[f_col0:f_col0 + A_HEADS, :])
        r_max_all = jnp.max(r_all, axis=1, keepdims=True)
        b_tot_all = cum_r[f_col0:f_col0 + A_HEADS, last:last + 1]
        m_prev_rows = m_scr[...]
        m_prev_all = m_prev_rows[:, 0:1]
        m_new_all = b_tot_all + jnp.maximum(m_prev_all, r_max_all)
        m_scr[...] = jnp.broadcast_to(m_new_all, m_scr.shape)
        decay_all = jnp.exp2(b_tot_all + m_prev_all - m_new_all)
        gain_all = jnp.exp2(b_tot_all + r_max_all - m_new_all)
        rp_all = jnp.maximum(r_all, m_prev_rows)
        for p in range(A_HEADS // 2):
            cols = slice(p * PAIR, (p + 1) * PAIR)
            q2 = q_ref[0, rows, cols]
            v2 = v_ref[0, rows, cols]
            kt2 = kt_ref[0, cols, rows]
            kt2f = kt2.astype(F32)
            tots, floors = [], []
            for half in range(2):
                hd = 2 * p + half
                mine = low if half == 0 else jnp.logical_not(low)
                qm = jnp.where(mine, q2, jnp.zeros_like(q2))
                vx = jnp.where(mine, v2, one)
                r_r = r_all[hd:hd + 1, :]
                b_c = cum_c[:, f_col0 + hd:f_col0 + hd + 1]
                c_ext = c_scr[hd]
                a = jnp.max(jnp.where(causal, rp_all[hd:hd + 1, :], -jnp.inf), axis=1, keepdims=True)
                sc = _dot(qm, kt2) * jnp.exp2(jnp.where(causal, r_r, -jnp.inf) - a)
                tots.append(_dot(sc.astype(BF16), vx)
                            + jnp.exp2(m_prev_rows[hd:hd + 1, :] - a) * _dot(qm, c_ext.astype(BF16)))
                floors.append(jnp.exp2(-(b_c + a)))
                upd = _dot((kt2f * jnp.exp2(r_r - r_max_all[hd:hd + 1, :])).astype(BF16), vx)
                c_scr[hd] = decay_all[hd:hd + 1, :] * c_ext + gain_all[hd:hd + 1, :] * upd
            den = pltpu.roll(jnp.where(low, tots[1], tots[0]), HEAD_DIM, axis=1)
            floor = jnp.where(low, floors[0], floors[1])
            h2 = jnp.where(low, tots[0], tots[1]) * (1.0 / jnp.maximum(jnp.abs(den), floor))
            if not reverse:
                out_ref[0, rows, cols] = h2
            else:
                hs = h2 + hf_ref[0, rows, cols]
                sq = hs * hs
                ms0 = jnp.sum(jnp.where(low, sq, 0.0), axis=1, keepdims=True) / HEAD_DIM
                ms1 = jnp.sum(jnp.where(low, 0.0, sq), axis=1, keepdims=True) / HEAD_DIM
                rs = jnp.where(low, lax.rsqrt(ms0 + EPS), lax.rsqrt(ms1 + EPS))
                y = hs * rs * ng_ref[:, cols]
                out_ref[0, rows, cols] = (jax.nn.sigmoid(oa_ref[0, rows, cols].astype(F32)) * y
                                          ).astype(BF16)


def _mlstm(q, kt, v, gates, gates_t, *, reverse, h_fwd=None, o_gate=None, norm_g=None):
    bsz, s, _ = q.shape
    n_chunks = min(A_CHUNKS_PER_STEP, s // A_CHUNK)
    tl = n_chunks * A_CHUNK
    nb = s // tl
    pos = (lambda c: nb - 1 - c) if reverse else (lambda c: c)
    tok = lambda width: pl.BlockSpec((1, tl, width), lambda b, c: (b, pos(c), 0))
    tok_t = lambda width: pl.BlockSpec((1, width, tl), lambda b, c: (b, 0, pos(c)))
    in_specs = [tok(A_WIDTH), tok_t(A_WIDTH), tok(A_WIDTH), tok(N_GATES), tok_t(N_GATES)]
    args = [q, kt, v, gates, gates_t]
    if reverse:
        in_specs += [tok(A_WIDTH), tok(A_WIDTH), _const_spec((1, A_WIDTH))]
        args += [h_fwd, o_gate, norm_g]
    return pl.pallas_call(
        functools.partial(_mlstm_kernel, reverse=reverse, n_chunks=n_chunks),
        grid=(bsz, nb),
        in_specs=in_specs,
        out_specs=tok(A_WIDTH),
        out_shape=jax.ShapeDtypeStruct((bsz, s, A_WIDTH), BF16 if reverse else F32),
        scratch_shapes=[pltpu.VMEM((A_HEADS, PAIR, PAIR), F32), pltpu.VMEM((A_HEADS, 128), F32)],
        compiler_params=_cparams("parallel", "arbitrary"),
        name="mlstm_bwd" if reverse else "mlstm_fwd",
    )(*args)


def _diff_tile_range(tq, tk):
    unit = math.gcd(tq, tk)
    e_hi = pl.cdiv(MAX_DISTANCE + tq - 1, unit)
    e_lo = -pl.cdiv(MAX_DISTANCE + tk - 1, unit)
    return unit, e_lo, e_hi


def _diff_kernel(lam_ref, qt_ref, k_ref, vt_ref, bias_ref, g_ref, o_ref,
                 s0, s1, p0, p1, a0, a1, x0, x1, m_scr, acc_scr, *, tk, unit, e_lo, e_hi, lam_init):
    tq = qt_ref.shape[2]
    nk = k_ref.shape[1] // tk
    q0 = pl.program_id(2) * tq
    qt = qt_ref[0]
    first = lax.broadcasted_iota(jnp.int32, qt.shape, 0) < HEAD_DIM
    zero = jnp.zeros_like(qt)
    q1t = jnp.where(first, qt, zero)
    q2t = jnp.where(first, zero, qt)
    ones_rows = jnp.ones((DIFF_ONES_ROWS, tk), BF16)
    m_scr[...] = jnp.full(m_scr.shape, -jnp.inf, F32)
    acc_scr[...] = jnp.zeros(acc_scr.shape, F32)

    def scores(kb, s_ref, x_ref):
        k0 = pl.multiple_of(kb * tk, tk)
        e = jnp.clip((k0 - q0) // unit, e_lo, e_hi) - e_lo
        k = k_ref[0, pl.ds(k0, tk), :]
        for half, qh in enumerate((q1t, q2t)):
            sc = _dot(k, qh) + bias_ref[0, e]
            s_ref[:, half * tq:(half + 1) * tq] = sc
            x_ref[:, half * tq:(half + 1) * tq] = jnp.max(sc, axis=0, keepdims=True)

    def softmax(s_ref, x_ref, p_ref, a_ref):
        m_old = m_scr[...]
        m_new = jnp.maximum(m_old, x_ref[...])
        a_ref[...] = jnp.exp2(m_old - m_new)
        m_scr[...] = m_new
        p_ref[...] = jnp.exp2(s_ref[...] - m_new).astype(BF16)

    def accumulate(kb, p_ref, a_ref):
        k0 = pl.multiple_of(kb * tk, tk)
        vx = jnp.concatenate([vt_ref[0, :, pl.ds(k0, tk)], ones_rows], axis=0)
        acc_scr[...] = a_ref[...] * acc_scr[...] + _dot(vx, p_ref[...])

    scores(0, s0, x0)
    scores(1, s1, x1)
    softmax(s0, x0, p0, a0)

    def steady(j, carry):
        t = 2 * j + 1
        scores(t + 1, s0, x0)
        softmax(s1, x1, p1, a1)
        accumulate(t - 1, p0, a0)
        scores(t + 2, s1, x1)
        softmax(s0, x0, p0, a0)
        accumulate(t, p1, a1)
        return carry

    lax.fori_loop(0, (nk - 2) // 2, steady, 0)
    softmax(s1, x1, p1, a1)
    accumulate(nk - 2, p0, a0)
    accumulate(nk - 1, p1, a1)

    acc = acc_scr[...]
    o = acc[:PAIR] * (1.0 / acc[PAIR:PAIR + 1])
    lam = lam_ref[...]
    lam_full = (jnp.exp(jnp.sum(lam[0:1] * lam[1:2], axis=1, keepdims=True))
                - jnp.exp(jnp.sum(lam[2:3] * lam[3:4], axis=1, keepdims=True)) + lam_init)
    o = o[:, :tq] - lam_full * o[:, tq:]
    inv = lax.rsqrt(jnp.mean(o * o, axis=0, keepdims=True) + EPS)
    y = o * inv * g_ref[...] * (1.0 - lam_init)
    o_ref[0] = y.T.astype(BF16)


def _diff_attention(qt, k, vt, lam, g, bias_tiles, layer_idx):
    bsz, s, _ = k.shape
    tq, tk = min(DIFF_TQ, s), min(DIFF_TK, s)
    assert (s // tk) % 2 == 0, "the key-block pipeline is unrolled by two"
    unit, e_lo, e_hi = _diff_tile_range(tq, tk)
    n_tiles = e_hi - e_lo + 1
    lam_init = 0.8 - 0.6 * math.exp(-0.3 * layer_idx)
    acc_rows = PAIR + DIFF_ONES_ROWS
    return pl.pallas_call(
        functools.partial(_diff_kernel, tk=tk, unit=unit, e_lo=e_lo, e_hi=e_hi, lam_init=lam_init),
        grid=(bsz, B_HEADS, s // tq),
        in_specs=[_const_spec(lam.shape),
                  pl.BlockSpec((1, PAIR, tq), lambda b, h, i: (b, h, i)),
                  pl.BlockSpec((1, s, PAIR), lambda b, h, i: (b, 0, h)),
                  pl.BlockSpec((1, PAIR, s), lambda b, h, i: (b, h, 0)),
                  pl.BlockSpec((1, n_tiles, tk, tq), lambda b, h, i: (h, 0, 0, 0)),
                  pl.BlockSpec((PAIR, 1), lambda b, h, i: (h, 0))],
        out_specs=pl.BlockSpec((1, tq, PAIR), lambda b, h, i: (b, i, h)),
        out_shape=jax.ShapeDtypeStruct((bsz, s, B_WIDTH), BF16),
        scratch_shapes=[pltpu.VMEM((tk, 2 * tq), F32), pltpu.VMEM((tk, 2 * tq), F32),
                        pltpu.VMEM((tk, 2 * tq), BF16), pltpu.VMEM((tk, 2 * tq), BF16),
                        pltpu.VMEM((1, 2 * tq), F32), pltpu.VMEM((1, 2 * tq), F32),
                        pltpu.VMEM((1, 2 * tq), F32), pltpu.VMEM((1, 2 * tq), F32),
                        pltpu.VMEM((1, 2 * tq), F32), pltpu.VMEM((acc_rows, 2 * tq), F32)],
        compiler_params=_cparams("parallel", "parallel", "arbitrary"),
        name="diff_attention",
    )(lam, qt, k, vt, bias_tiles, g)


def _diff_bias_tiles(table, s):
    tq, tk = min(DIFF_TQ, s), min(DIFF_TK, s)
    unit, e_lo, e_hi = _diff_tile_range(tq, tk)
    return _bias_tiles(table, n_heads=B_HEADS, n_tiles=e_hi - e_lo + 1, tq=tq, tk=tk, col0=B_COL0,
                       offset0=e_lo * unit, stride=unit, dil=1, band=None, scale=LOG2E,
                       keys_on_rows=True)


def _local_kernel(*refs, hw, ti, k_blocks, with_sink, with_lse):
    refs = list(refs)
    sink_ref = refs.pop(0) if with_sink else None
    (q_ref, kp_ref, kc_ref, kn_ref, vp_ref, vc_ref, vn_ref, bias_ref) = refs[:8]
    o_ref = refs[8]
    lse_ref = refs[9] if with_lse else None
    kext, vext = refs[-2:]
    tb = q_ref.shape[1]
    i = pl.program_id(2)
    seq = pl.num_programs(2) * tb
    kext[0:hw, :] = kp_ref[0]
    kext[hw:hw + tb, :] = kc_ref[0]
    kext[hw + tb:, :] = kn_ref[0]
    vext[0:hw, :] = vp_ref[0]
    vext[hw:hw + tb, :] = vc_ref[0]
    vext[hw + tb:, :] = vn_ref[0]
    wk = ti + 2 * hw
    n_sub = tb // ti
    low = lax.broadcasted_iota(jnp.int32, (ti, PAIR), 1) < HEAD_DIM
    kj = lax.broadcasted_iota(jnp.int32, (1, wk), 1)
    for j in range(n_sub):
        rows = slice(j * ti, (j + 1) * ti)
        krows = slice(j * ti, j * ti + wk)
        edge = None
        if j in (0, n_sub - 1):
            kpos = i * tb + j * ti - hw + kj
            edge = jnp.where((kpos >= 0) & (kpos < seq), 0.0, NEG)
        for kb in sorted(set(k_blocks)):
            pairs = [p for p, b in enumerate(k_blocks) if b == kb]
            kk = kext[krows, kb * PAIR:(kb + 1) * PAIR]
            vv = vext[krows, kb * PAIR:(kb + 1) * PAIR]
            stack = []
            for p in pairs:
                q2 = q_ref[0, rows, p * PAIR:(p + 1) * PAIR]
                zero = jnp.zeros_like(q2)
                stack += [jnp.where(low, q2, zero), jnp.where(low, zero, q2)]
            sc_all = _dot_nt(jnp.concatenate(stack, axis=0), kk)
            probs, inv_dens, lses = [], [], []
            for n, p in enumerate(pairs):
                for half in range(2):
                    hd = 2 * p + half
                    sc = sc_all[(2 * n + half) * ti:(2 * n + half + 1) * ti] + bias_ref[hd]
                    if edge is not None:
                        sc = sc + edge
                    m = jnp.max(sc, axis=1, keepdims=True)
                    if with_sink:
                        m = jnp.maximum(m, sink_ref[hd])
                    pr = jnp.exp(sc - m)
                    den = jnp.sum(pr, axis=1, keepdims=True)
                    if with_sink:
                        den = den + jnp.exp(sink_ref[hd] - m)
                    probs.append(pr.astype(BF16))
                    inv_dens.append(1.0 / den)
                    lses.append(m + jnp.log(den))
            o_all = _dot(jnp.concatenate(probs, axis=0), vv)
            for n, p in enumerate(pairs):
                o0 = o_all[2 * n * ti:(2 * n + 1) * ti] * inv_dens[2 * n]
                o1 = o_all[(2 * n + 1) * ti:(2 * n + 2) * ti] * inv_dens[2 * n + 1]
                o_ref[0, rows, p * PAIR:(p + 1) * PAIR] = jnp.where(low, o0, o1).astype(o_ref.dtype)
                if with_lse:
                    lse_ref[0, rows, p * PAIR:(p + 1) * PAIR] = jnp.where(low, lses[2 * n],
                                                                           lses[2 * n + 1])


def _local_attention(q_arr, k_arr, v_arr, bias, *, seq, n_res, q_width, q_col, k_col, v_col, hw,
                     k_blocks, out_dtype, sink=None, with_lse=False):
    bsz = q_arr.shape[0]
    tb = min(LOCAL_OUTER, seq)
    ti = min(LOCAL_TILE, tb)
    kw = len(set(k_blocks)) * PAIR
    hb = tb // hw
    n_halo = seq // hw
    cur = lambda width, colf: pl.BlockSpec((1, tb, width), lambda b, r, i: (b, i, colf(r)))
    prev = lambda colf: pl.BlockSpec((1, hw, kw), lambda b, r, i: (b, jnp.maximum(i * hb - 1, 0), colf(r)))
    nxt = lambda colf: pl.BlockSpec(
        (1, hw, kw), lambda b, r, i: (b, jnp.minimum((i + 1) * hb, n_halo - 1), colf(r)))
    in_specs = [cur(q_width, q_col), prev(k_col), cur(kw, k_col), nxt(k_col),
                prev(v_col), cur(kw, v_col), nxt(v_col), _const_spec(bias.shape)]
    args = [q_arr, k_arr, k_arr, k_arr, v_arr, v_arr, v_arr, bias]
    if sink is not None:
        in_specs.insert(0, pl.BlockSpec(memory_space=pltpu.SMEM))
        args.insert(0, sink)
    out_spec = pl.BlockSpec((1, tb, q_width), lambda b, r, i: (b, i, r))
    out_sds = jax.ShapeDtypeStruct((bsz, seq, n_res * q_width), out_dtype)
    out_specs, out_shape = [out_spec], [out_sds]
    if with_lse:
        out_specs.append(out_spec)
        out_shape.append(jax.ShapeDtypeStruct((bsz, seq, n_res * q_width), F32))
    return pl.pallas_call(
        functools.partial(_local_kernel, hw=hw, ti=ti, k_blocks=tuple(k_blocks),
                          with_sink=sink is not None, with_lse=with_lse),
        grid=(bsz, n_res, seq // tb),
        in_specs=in_specs,
        out_specs=out_specs,
        out_shape=out_shape,
        scratch_shapes=[pltpu.VMEM((tb + 2 * hw, kw), BF16), pltpu.VMEM((tb + 2 * hw, kw), BF16)],
        compiler_params=_cparams("parallel", "parallel", "parallel"),
        name="local_attention",
    )(*args)


def _post_kernel(*refs, even):
    refs = list(refs)
    if even:
        a_ref, b_ref = refs[:2]
        rest = refs[2:]
    else:
        c_refs = refs[:6]
        d_ref = refs[6]
        rest = refs[7:]
    x_ref, mod_ref, gains_ref, wo_ref, w1_ref, w2_ref, out_ref = rest
    if even:
        y = _dot(a_ref[0], wo_ref[0:A_WIDTH, :]) + _dot(b_ref[0], wo_ref[A_WIDTH:, :])
    else:
        os_ = [c_refs[2 * g][0] for g in range(3)]
        ls_ = [c_refs[2 * g + 1][0] for g in range(3)]
        mx = jnp.maximum(jnp.maximum(ls_[0], ls_[1]), ls_[2])
        es = [jnp.exp(l - mx) for l in ls_]
        inv = 1.0 / (es[0] + es[1] + es[2])
        y = _dot(d_ref[0], wo_ref[C_WIDTH:, :])
        for g in range(3):
            y = y + _dot((os_[g] * (es[g] * inv)).astype(BF16), wo_ref[g * C_GW:(g + 1) * C_GW, :])
    gate1 = mod_ref[0, 2:3, :]
    shift2 = mod_ref[0, 3:4, :]
    scale2 = mod_ref[0, 4:5, :]
    gate2 = mod_ref[0, 5:6, :]
    x = x_ref[0] + gate1 * _rms(y, gains_ref[0:1, :])
    h = _prenorm(x, gains_ref[1:2, :], shift2, scale2)
    acc = jnp.zeros_like(x)
    step = D_MODEL
    for j in range(D_FF // step):
        u = jnp.maximum(_dot(h, w1_ref[:, j * step:(j + 1) * step]), 0.0)
        acc = acc + _dot((u * u).astype(BF16), w2_ref[j * step:(j + 1) * step, :])
    out_ref[0] = x + gate2 * _rms(acc, gains_ref[2:3, :])


def _post_mixer(mix, x, mod, gains, wo, w1, w2, *, even):
    bsz, s, d = x.shape
    tm = min(TOKEN_TILE, s)
    tok = lambda width: pl.BlockSpec((1, tm, width), lambda b, i: (b, i, 0))
    resident = lambda a: pl.BlockSpec(a.shape, lambda b, i: (0, 0), pipeline_mode=pl.Buffered(1))
    in_specs = [tok(m.shape[-1]) for m in mix]
    in_specs += [tok(d), pl.BlockSpec((1, 6, d), lambda b, i: (b, 0, 0)), _const_spec(gains.shape),
                 resident(wo), resident(w1), resident(w2)]
    return pl.pallas_call(
        functools.partial(_post_kernel, even=even),
        grid=(bsz, s // tm),
        in_specs=in_specs,
        out_specs=tok(d),
        out_shape=jax.ShapeDtypeStruct((bsz, s, d), F32),
        compiler_params=_cparams("parallel", "parallel"),
        name="post_mixer_mlp",
    )(*mix, x, mod, gains, wo, w1, w2)


def _even_layer(x, mod, layer_idx, p, diff_tiles):
    e = layer_idx // 2
    w_in = p["even_w_in"][e]
    o1 = 2 * A_WIDTH
    o2 = o1 + 2 * A_WIDTH
    o3 = o2 + N_GATES
    w = {"qk": w_in[:, :o1].astype(BF16), "vo": w_in[:, o1:o2].astype(BF16),
         "g": w_in[:, o2:o3].astype(BF16), "b": w_in[:, o3:].astype(BF16)}
    qa, ka, va, oa, gates, qb, kb, vb = _even_in_proj(
        x, mod, p["mix_pre_g"][layer_idx][None], w, p["even_conv_w"][e],
        p["even_conv_b"][e][None], p["mlstm_gate_b"][e].reshape(1, N_GATES))
    gates_t = jnp.swapaxes(gates, 1, 2)
    ka_t = jnp.swapaxes(ka, 1, 2)
    h_fwd = _mlstm(qa, ka_t, va, gates, gates_t, reverse=False)
    mix_a = _mlstm(qa, ka_t, va, gates, gates_t, reverse=True, h_fwd=h_fwd, o_gate=oa,
                   norm_g=p["mlstm_norm_g"][e][None])
    mix_b = _diff_attention(jnp.swapaxes(qb, 1, 2), kb, jnp.swapaxes(vb, 1, 2), p["diff_lambda"][e],
                            p["diff_norm_g"][e][:, None], diff_tiles, layer_idx)
    return [mix_a, mix_b], p["even_w_out"][e].astype(BF16)


def _odd_layer(x, mod, layer_idx, p, c_tiles, d_tiles):
    o = layer_idx // 2
    bsz, s, _ = x.shape
    w_in = p["odd_w_in"][o]
    cw = C_WIDTH
    wq, wk, wv = w_in[:, :cw] * QK_SCALE, w_in[:, cw:2 * cw], w_in[:, 2 * cw:3 * cw]
    wc = jnp.stack([jnp.concatenate([t[:, g * C_GW:(g + 1) * C_GW] for t in (wq, wk, wv)], axis=1)
                    for g in range(len(C_PAIRS))]).astype(BF16)
    d0 = 3 * cw
    wqd = w_in[:, d0:d0 + D_WIDTH] * QK_SCALE
    wkd = w_in[:, d0 + D_WIDTH:d0 + D_WIDTH + D_KV_HEADS * HEAD_DIM]
    wvd = w_in[:, d0 + D_WIDTH + D_KV_HEADS * HEAD_DIM:]
    dup = lambda t: jnp.concatenate([t[:, g * HEAD_DIM:(g + 1) * HEAD_DIM]
                                     for g in range(D_KV_HEADS) for _ in range(2)], axis=1)
    wd = jnp.concatenate([wqd, dup(wkd), dup(wvd)], axis=1).astype(BF16)
    c0, c1, c2, qd, kd, vd = _odd_in_proj(x, mod, p["mix_pre_g"][layer_idx][None], {"c": wc, "d": wd})
    mix = []
    for g, (qkv, (window, dil)) in enumerate(zip((c0, c1, c2), C_PAIRS)):
        sub = s // dil
        strided = qkv.reshape(bsz, sub, dil * 3 * C_GW)
        o_g, lse_g = _local_attention(
            strided, strided, strided, c_tiles[g], seq=sub, n_res=dil, q_width=C_GW,
            q_col=lambda r: 3 * r, k_col=lambda r: 3 * r + 1, v_col=lambda r: 3 * r + 2,
            hw=window // (2 * dil), k_blocks=(0, 1), out_dtype=BF16, with_lse=True)
        mix += [o_g.reshape(bsz, s, C_GW), lse_g.reshape(bsz, s, C_GW)]
    o_d = _local_attention(qd, kd, vd, d_tiles, seq=s, n_res=1, q_width=D_WIDTH,
                           q_col=lambda r: 0, k_col=lambda r: 0, v_col=lambda r: 0,
                           hw=D_HALF_WINDOW, k_blocks=(0, 0, 1, 1), out_dtype=BF16,
                           sink=p["sink_logit"][o])[0]
    mix.append(o_d)
    return mix, p["odd_w_out"][o].astype(BF16)


def _local_bias_tiles(table, seq, *, n_heads, col0, hw, dil):
    ti = min(LOCAL_TILE, LOCAL_OUTER, seq)
    return _bias_tiles(table, n_heads=n_heads, n_tiles=1, tq=ti, tk=ti + 2 * hw, col0=col0,
                       offset0=-hw, stride=0, dil=dil, band=hw)[:, 0]


def _trunk(x, mods, p):
    bsz, s, _ = x.shape
    table = p["rel_bias_table"]
    diff_tiles = _diff_bias_tiles(table, s)
    c_tiles = [_local_bias_tiles(table, s // dil, n_heads=C_HPG, col0=C_COL0 + g * C_HPG,
                                 hw=window // (2 * dil), dil=dil)
               for g, (window, dil) in enumerate(C_PAIRS)]
    d_tiles = _local_bias_tiles(table, s, n_heads=D_HEADS, col0=D_COL0, hw=D_HALF_WINDOW, dil=1)
    for l in range(DEPTH):
        mod = mods[l].reshape(bsz, 6, D_MODEL)
        if l % 2 == 0:
            mix, wo = _even_layer(x, mod, l, p, diff_tiles)
        else:
            mix, wo = _odd_layer(x, mod, l, p, c_tiles, d_tiles)
        gains = jnp.stack([p["mix_post_g"][l], p["mlp_pre_g"][l], p["mlp_post_g"][l]])
        x = _post_mixer(mix, x, mod, gains, wo, p["mlp_w1"][l].astype(BF16),
                        p["mlp_w2"][l].astype(BF16), even=l % 2 == 0)
    return x


def kernel(x_prompt, x_sample, c_prompt, c_sample, rel_bias_table, ada_w, ada_b, mix_pre_g, mix_post_g, mlp_pre_g, mlp_post_g, mlp_w1, mlp_w2, even_w_in, even_conv_w, even_conv_b, mlstm_gate_b, mlstm_norm_g, diff_lambda, diff_norm_g, even_w_out, odd_w_in, odd_w_out, sink_logit):
    p = dict(rel_bias_table=rel_bias_table, mix_pre_g=mix_pre_g, mix_post_g=mix_post_g,
             mlp_pre_g=mlp_pre_g, mlp_post_g=mlp_post_g, mlp_w1=mlp_w1, mlp_w2=mlp_w2,
             even_w_in=even_w_in, even_conv_w=even_conv_w, even_conv_b=even_conv_b,
             mlstm_gate_b=mlstm_gate_b, mlstm_norm_g=mlstm_norm_g, diff_lambda=diff_lambda,
             diff_norm_g=diff_norm_g, even_w_out=even_w_out, odd_w_in=odd_w_in,
             odd_w_out=odd_w_out, sink_logit=sink_logit)
    n_prompt = c_prompt.shape[0]
    mods = _modulation(jnp.concatenate([c_prompt, c_sample], axis=0), ada_w, ada_b)
    y_prompt = _trunk(x_prompt, mods[:, :n_prompt], p)
    y_sample = _trunk(x_sample, mods[:, n_prompt:], p)
    return (y_prompt, y_sample)
```

```python
import functools
import math

import jax
import jax.numpy as jnp
from jax import lax
from jax.experimental import pallas as pl
from jax.experimental.pallas import tpu as pltpu

F32 = jnp.float32
BF16 = jnp.bfloat16

D_MODEL = 1024
DEPTH = 4
HEAD_DIM = 64
PAIR = 2 * HEAD_DIM
EPS = 1e-6
NEG = -1e30
QK_SCALE = HEAD_DIM ** -0.5
LOG2E = math.log2(math.e)
A_HEADS = D_MODEL // 128
A_WIDTH = A_HEADS * HEAD_DIM
A_CONV = 5
A_CHUNK = 128
A_CHUNKS_PER_STEP = 4
N_GATES = 4 * A_HEADS
B_HEADS = D_MODEL // 256
B_WIDTH = B_HEADS * 2 * HEAD_DIM
C_PAIRS = ((128, 1), (512, 4), (2048, 16))
C_HPG = D_MODEL // 256
C_GW = C_HPG * HEAD_DIM
C_WIDTH = len(C_PAIRS) * C_GW
D_HEADS = D_MODEL // 128
D_KV_HEADS = D_HEADS // 4
D_WIDTH = D_HEADS * HEAD_DIM
D_HALF_WINDOW = 128
D_FF = 4 * D_MODEL
N_BUCKETS = 32
MAX_DISTANCE = 1024
B_COL0 = 0
C_COL0 = B_HEADS
D_COL0 = B_HEADS + len(C_PAIRS) * C_HPG

VMEM_LIMIT = 56 * 1024 * 1024
CONV_HALO = 16
TOKEN_TILE = 512
DIFF_TQ = 1024
DIFF_TK = 512
DIFF_ONES_ROWS = 16
LOCAL_TILE = 128
LOCAL_OUTER = 512
LOCAL_OUT_ROWS = 4096


def _cparams(*sem):
    return pltpu.CompilerParams(dimension_semantics=sem, vmem_limit_bytes=VMEM_LIMIT)


def _dot(a, b):
    return jnp.dot(a, b, preferred_element_type=F32)


def _dot_nt(a, b):
    return lax.dot_general(a, b, (((1,), (1,)), ((), ())), preferred_element_type=F32)


def _dot_tn(a, b):
    return lax.dot_general(a, b, (((0,), (0,)), ((), ())), preferred_element_type=F32)


def _rms(x, g):
    return x * lax.rsqrt(jnp.mean(x * x, axis=-1, keepdims=True) + EPS) * g


def _const_spec(shape):
    zeros = (0,) * len(shape)
    return pl.BlockSpec(shape, lambda *_: zeros)


def _rel_bucket(rel):
    half = N_BUCKETS // 2
    max_exact = half // 2
    ret = jnp.where(rel > 0, half, 0)
    n = jnp.abs(rel)
    nf = jnp.maximum(n, 1).astype(F32)
    large = max_exact + (jnp.log(nf / max_exact) / math.log(MAX_DISTANCE / max_exact)
                         * (half - max_exact)).astype(jnp.int32)
    large = jnp.minimum(large, half - 1)
    return ret + jnp.where(n < max_exact, n, large)


def _bias_tile_kernel(table_ref, out_ref, *, col0, offset0, stride, dil, band, scale, keys_on_rows):
    h = pl.program_id(0)
    t = pl.program_id(1)
    shape = out_ref.shape[2:]
    qi = lax.broadcasted_iota(jnp.int32, shape, 1 if keys_on_rows else 0)
    ki = lax.broadcasted_iota(jnp.int32, shape, 0 if keys_on_rows else 1)
    delta = offset0 + t * stride + ki - qi
    bucket = _rel_bucket(delta * dil)
    val = jnp.zeros(shape, F32)
    for j in range(N_BUCKETS):
        val = jnp.where(bucket == j, table_ref[j, col0 + h], val)
    if scale != 1.0:
        val = val * scale
    if band is not None:
        val = jnp.where(jnp.abs(delta) <= band, val, NEG)
    out_ref[0, 0] = val


def _bias_tiles(table, *, n_heads, n_tiles, tq, tk, col0, offset0, stride, dil, band, scale=1.0,
                keys_on_rows=False):
    shape = (tk, tq) if keys_on_rows else (tq, tk)
    return pl.pallas_call(
        functools.partial(_bias_tile_kernel, col0=col0, offset0=offset0, stride=stride, dil=dil,
                          band=band, scale=scale, keys_on_rows=keys_on_rows),
        grid=(n_heads, n_tiles),
        in_specs=[pl.BlockSpec(memory_space=pltpu.SMEM)],
        out_specs=pl.BlockSpec((1, 1) + shape, lambda h, t: (h, t, 0, 0)),
        out_shape=jax.ShapeDtypeStruct((n_heads, n_tiles) + shape, F32),
        compiler_params=_cparams("parallel", "parallel"),
        name="rel_bias_tiles",
    )(table)


def _mod_kernel(c_ref, w_ref, b_ref, o_ref):
    c = c_ref[...]
    a = (c * jax.nn.sigmoid(c)).astype(BF16)
    o_ref[0] = _dot(a, w_ref[0].astype(BF16)) + b_ref[0]


def _modulation(c_all, ada_w, ada_b):
    n, d = c_all.shape
    depth, _, width = ada_w.shape
    tn = 1536
    return pl.pallas_call(
        _mod_kernel,
        grid=(depth, width // tn),
        in_specs=[_const_spec((n, d)),
                  pl.BlockSpec((1, d, tn), lambda l, j: (l, 0, j)),
                  pl.BlockSpec((1, 1, tn), lambda l, j: (l, 0, j))],
        out_specs=pl.BlockSpec((1, n, tn), lambda l, j: (l, 0, j)),
        out_shape=jax.ShapeDtypeStruct((depth, n, width), F32),
        compiler_params=_cparams("parallel", "parallel"),
        name="adaln_modulation",
    )(c_all, ada_w, ada_b.reshape(depth, 1, width))


def _prenorm(x, g, shift, scale):
    return (_rms(x, g) * (1.0 + scale) + shift).astype(BF16)


def _even_in_kernel(x_ref, xp_ref, xn_ref, mod_ref, g_ref, wqk_ref, wvo_ref, wg_ref, wb_ref,
                    cw_ref, cb_ref, gb_ref,
                    qa_ref, ka_ref, va_ref, oa_ref, gates_ref, qb_ref, kb_ref, vb_ref, hx_ref, ext_ref):
    i = pl.program_id(1)
    n = pl.num_programs(1)
    tm = x_ref.shape[1]
    g = g_ref[...]
    shift = mod_ref[0, 0:1, :]
    scale = mod_ref[0, 1:2, :]
    h = _prenorm(x_ref[0], g, shift, scale)
    hx_ref[0:CONV_HALO, :] = _prenorm(xp_ref[0], g, shift, scale)
    hx_ref[CONV_HALO:CONV_HALO + tm, :] = h
    hx_ref[CONV_HALO + tm:, :] = _prenorm(xn_ref[0], g, shift, scale)
    ext_ref[...] = _dot(hx_ref[...], wqk_ref[...])
    ext_ref[0:CONV_HALO, :] = ext_ref[0:CONV_HALO, :] * (i > 0).astype(F32)
    ext_ref[CONV_HALO + tm:, :] = ext_ref[CONV_HALO + tm:, :] * (i < n - 1).astype(F32)
    acc = jnp.broadcast_to(cb_ref[...], (tm, 2 * A_WIDTH))
    for j in range(A_CONV):
        start = CONV_HALO - A_CONV // 2 + j
        acc = acc + cw_ref[j:j + 1, :] * ext_ref[start:start + tm, :]
    qk = acc * jax.nn.sigmoid(acc)
    qa_ref[0] = qk[:, :A_WIDTH].astype(BF16)
    ka_ref[0] = (qk[:, A_WIDTH:] * QK_SCALE).astype(BF16)
    vo = _dot(h, wvo_ref[...])
    va_ref[0] = vo[:, :A_WIDTH].astype(BF16)
    oa_ref[0] = vo[:, A_WIDTH:].astype(BF16)
    gates_ref[0] = _dot(h, wg_ref[...]) + gb_ref[...]
    qkv = _dot(h, wb_ref[...])
    qb_ref[0] = (qkv[:, :B_WIDTH] * (QK_SCALE * LOG2E)).astype(BF16)
    kb_ref[0] = qkv[:, B_WIDTH:2 * B_WIDTH].astype(BF16)
    vb_ref[0] = qkv[:, 2 * B_WIDTH:].astype(BF16)


def _even_in_proj(x, mod, g, w, conv_w, conv_b, gate_b):
    bsz, s, d = x.shape
    tm = min(TOKEN_TILE, s)
    nt = s // tm
    hb = tm // CONV_HALO
    n_halo = s // CONV_HALO
    tok = lambda width: pl.BlockSpec((1, tm, width), lambda b, i: (b, i, 0))
    outs = [A_WIDTH, A_WIDTH, A_WIDTH, A_WIDTH, N_GATES, B_WIDTH, B_WIDTH, B_WIDTH]
    dts = [BF16, BF16, BF16, BF16, F32, BF16, BF16, BF16]
    return pl.pallas_call(
        _even_in_kernel,
        grid=(bsz, nt),
        in_specs=[tok(d),
                  pl.BlockSpec((1, CONV_HALO, d), lambda b, i: (b, jnp.maximum(i * hb - 1, 0), 0)),
                  pl.BlockSpec((1, CONV_HALO, d),
                               lambda b, i: (b, jnp.minimum((i + 1) * hb, n_halo - 1), 0)),
                  pl.BlockSpec((1, 6, d), lambda b, i: (b, 0, 0)),
                  _const_spec((1, d)),
                  _const_spec(w["qk"].shape), _const_spec(w["vo"].shape),
                  _const_spec(w["g"].shape), _const_spec(w["b"].shape),
                  _const_spec(conv_w.shape), _const_spec(conv_b.shape), _const_spec(gate_b.shape)],
        out_specs=[tok(width) for width in outs],
        out_shape=[jax.ShapeDtypeStruct((bsz, s, width), dt) for width, dt in zip(outs, dts)],
        scratch_shapes=[pltpu.VMEM((tm + 2 * CONV_HALO, d), BF16),
                        pltpu.VMEM((tm + 2 * CONV_HALO, 2 * A_WIDTH), F32)],
        compiler_params=_cparams("parallel", "parallel"),
        name="even_in_proj",
    )(x, x, x, mod, g, w["qk"], w["vo"], w["g"], w["b"], conv_w, conv_b, gate_b)


def _odd_in_kernel(x_ref, mod_ref, g_ref, wc_ref, wd_ref, c0_ref, c1_ref, c2_ref, qd_ref, kd_ref, vd_ref):
    h = _prenorm(x_ref[0], g_ref[...], mod_ref[0, 0:1, :], mod_ref[0, 1:2, :])
    for g, ref in enumerate((c0_ref, c1_ref, c2_ref)):
        ref[0] = _dot(h, wc_ref[g]).astype(BF16)
    d = _dot(h, wd_ref[...])
    qd_ref[0] = d[:, :D_WIDTH].astype(BF16)
    kd_ref[0] = d[:, D_WIDTH:D_WIDTH + 2 * PAIR].astype(BF16)
    vd_ref[0] = d[:, D_WIDTH + 2 * PAIR:].astype(BF16)


def _odd_in_proj(x, mod, g, w):
    bsz, s, d = x.shape
    tm = min(TOKEN_TILE, s)
    tok = lambda width: pl.BlockSpec((1, tm, width), lambda b, i: (b, i, 0))
    outs = [3 * C_GW] * 3 + [D_WIDTH, 2 * PAIR, 2 * PAIR]
    return pl.pallas_call(
        _odd_in_kernel,
        grid=(bsz, s // tm),
        in_specs=[tok(d), pl.BlockSpec((1, 6, d), lambda b, i: (b, 0, 0)), _const_spec((1, d)),
                  _const_spec(w["c"].shape), _const_spec(w["d"].shape)],
        out_specs=[tok(width) for width in outs],
        out_shape=[jax.ShapeDtypeStruct((bsz, s, width), BF16) for width in outs],
        compiler_params=_cparams("parallel", "parallel"),
        name="odd_in_proj",
    )(x, mod, g, w["c"], w["d"])


def _log_sigmoid(x):
    return jnp.minimum(x, 0.0) - jnp.log1p(jnp.exp(-jnp.abs(x)))


def _mlstm_kernel(*refs, reverse, n_chunks):
    if reverse:
        (q_ref, kt_ref, v_ref, gc_ref, gr_ref, hf_ref, oa_ref, ng_ref, out_ref, c_scr, m_scr) = refs
    else:
        (q_ref, kt_ref, v_ref, gc_ref, gr_ref, out_ref, c_scr, m_scr) = refs
    L = A_CHUNK

    @pl.when(pl.program_id(1) == 0)
    def _():
        c_scr[...] = jnp.zeros_like(c_scr)
        m_scr[...] = jnp.zeros_like(m_scr)

    row = lax.broadcasted_iota(jnp.int32, (L, L), 0)
    col = lax.broadcasted_iota(jnp.int32, (L, L), 1)
    causal = (col >= row) if reverse else (col <= row)
    causal_t = (row >= col) if reverse else (row <= col)
    causal_f = causal.astype(F32)
    causal_tf = causal_t.astype(F32)
    logf_c = _log_sigmoid(gc_ref[0]) * LOG2E
    logf_r = _log_sigmoid(gr_ref[0]) * LOG2E
    i_col0 = 2 * A_HEADS if reverse else 0
    f_col0 = i_col0 + A_HEADS
    last = 0 if reverse else L - 1
    low = lax.broadcasted_iota(jnp.int32, (L, PAIR), 1) < HEAD_DIM
    one = jnp.ones((L, PAIR), BF16)

    for c in (range(n_chunks - 1, -1, -1) if reverse else range(n_chunks)):
        rows = slice(c * L, (c + 1) * L)
        cum_c = jnp.dot(causal_f, logf_c[rows], precision=lax.Precision.HIGHEST,
                        preferred_element_type=F32)
        cum_r = jnp.dot(logf_r[:, rows], causal_tf, precision=lax.Precision.HIGHEST,
                        preferred_element_type=F32)
        r_all = (gr_ref[0, i_col0:i_col0 + A_HEADS, rows] * LOG2E
                 - cum_r[f_col0:f_col0 + A_HEADS, :])
        r_max_all = jnp.max(r_all, axis=1, keepdims=True)
        b_tot_all = cum_r[f_col0:f_col0 + A_HEADS, last:last + 1]
        m_prev_rows = m_scr[...]
        m_prev_all = m_prev_rows[:, 0:1]
        m_new_all = b_tot_all + jnp.maximum(m_prev_all, r_max_all)
        m_scr[...] = jnp.broadcast_to(m_new_all, m_scr.shape)
        decay_all = jnp.exp2(b_tot_all + m_prev_all - m_new_all)
        gain_all = jnp.exp2(b_tot_all + r_max_all - m_new_all)
        rp_all = jnp.maximum(r_all, m_prev_rows)
        for p in range(A_HEADS // 2):
            cols = slice(p * PAIR, (p + 1) * PAIR)
            q2 = q_ref[0, rows, cols]
            v2 = v_ref[0, rows, cols]
            kt2 = kt_ref[0, cols, rows]
            kt2f = kt2.astype(F32)
            tots, floors = [], []
            for half in range(2):
                hd = 2 * p + half
                mine = low if half == 0 else jnp.logical_not(low)
                qm = jnp.where(mine, q2, jnp.zeros_like(q2))
                vx = jnp.where(mine, v2, one)
                r_r = r_all[hd:hd + 1, :]
                b_c = cum_c[:, f_col0 + hd:f_col0 + hd + 1]
                c_ext = c_scr[hd]
                a = jnp.max(jnp.where(causal, rp_all[hd:hd + 1, :], -jnp.inf), axis=1, keepdims=True)
                sc = _dot(qm, kt2) * jnp.exp2(jnp.where(causal, r_r, -jnp.inf) - a)
                tots.append(_dot(sc.astype(BF16), vx)
                            + jnp.exp2(m_prev_rows[hd:hd + 1, :] - a) * _dot(qm, c_ext.astype(BF16)))
                floors.append(jnp.exp2(-(b_c + a)))
                upd = _dot((kt2f * jnp.exp2(r_r - r_max_all[hd:hd + 1, :])).astype(BF16), vx)
                c_scr[hd] = decay_all[hd:hd + 1, :] * c_ext + gain_all[hd:hd + 1, :] * upd
            den = pltpu.roll(jnp.where(low, tots[1], tots[0]), HEAD_DIM, axis=1)
            floor = jnp.where(low, floors[0], floors[1])
            h2 = jnp.where(low, tots[0], tots[1]) * (1.0 / jnp.maximum(jnp.abs(den), floor))
            if not reverse:
                out_ref[0, rows, cols] = h2
            else:
                hs = h2 + hf_ref[0, rows, cols]
                sq = hs * hs
                ms0 = jnp.sum(jnp.where(low, sq, 0.0), axis=1, keepdims=True) / HEAD_DIM
                ms1 = jnp.sum(jnp.where(low, 0.0, sq), axis=1, keepdims=True) / HEAD_DIM
                rs = jnp.where(low, lax.rsqrt(ms0 + EPS), lax.rsqrt(ms1 + EPS))
                y = hs * rs * ng_ref[:, cols]
                out_ref[0, rows, cols] = (jax.nn.sigmoid(oa_ref[0, rows, cols].astype(F32)) * y
                                          ).astype(BF16)


def _mlstm(q, kt, v, gates, gates_t, *, reverse, h_fwd=None, o_gate=None, norm_g=None):
    bsz, s, _ = q.shape
    n_chunks = min(A_CHUNKS_PER_STEP, s // A_CHUNK)
    tl = n_chunks * A_CHUNK
    nb = s // tl
    pos = (lambda c: nb - 1 - c) if reverse else (lambda c: c)
    tok = lambda width: pl.BlockSpec((1, tl, width), lambda b, c: (b, pos(c), 0))
    tok_t = lambda width: pl.BlockSpec((1, width, tl), lambda b, c: (b, 0, pos(c)))
    in_specs = [tok(A_WIDTH), tok_t(A_WIDTH), tok(A_WIDTH), tok(N_GATES), tok_t(N_GATES)]
    args = [q, kt, v, gates, gates_t]
    if reverse:
        in_specs += [tok(A_WIDTH), tok(A_WIDTH), _const_spec((1, A_WIDTH))]
        args += [h_fwd, o_gate, norm_g]
    return pl.pallas_call(
        functools.partial(_mlstm_kernel, reverse=reverse, n_chunks=n_chunks),
        grid=(bsz, nb),
        in_specs=in_specs,
        out_specs=tok(A_WIDTH),
        out_shape=jax.ShapeDtypeStruct((bsz, s, A_WIDTH), BF16 if reverse else F32),
        scratch_shapes=[pltpu.VMEM((A_HEADS, PAIR, PAIR), F32), pltpu.VMEM((A_HEADS, 128), F32)],
        compiler_params=_cparams("parallel", "arbitrary"),
        name="mlstm_bwd" if reverse else "mlstm_fwd",
    )(*args)


def _diff_tile_range(tq, tk):
    unit = math.gcd(tq, tk)
    e_hi = pl.cdiv(MAX_DISTANCE + tq - 1, unit)
    e_lo = -pl.cdiv(MAX_DISTANCE + tk - 1, unit)
    return unit, e_lo, e_hi


def _diff_kernel(lam_ref, qt_ref, k_ref, vt_ref, bias_ref, g_ref, o_ref,
                 s0, s1, p0, p1, a0, a1, x0, x1, m_scr, acc_scr, *, tk, unit, e_lo, e_hi, lam_init):
    tq = qt_ref.shape[2]
    nk = k_ref.shape[1] // tk
    q0 = pl.program_id(2) * tq
    qt = qt_ref[0]
    first = lax.broadcasted_iota(jnp.int32, qt.shape, 0) < HEAD_DIM
    zero = jnp.zeros_like(qt)
    q1t = jnp.where(first, qt, zero)
    q2t = jnp.where(first, zero, qt)
    ones_rows = jnp.ones((DIFF_ONES_ROWS, tk), BF16)
    m_scr[...] = jnp.full(m_scr.shape, -jnp.inf, F32)
    acc_scr[...] = jnp.zeros(acc_scr.shape, F32)

    def scores(kb, s_ref, x_ref):
        k0 = pl.multiple_of(kb * tk, tk)
        e = jnp.clip((k0 - q0) // unit, e_lo, e_hi) - e_lo
        k = k_ref[0, pl.ds(k0, tk), :]
        for half, qh in enumerate((q1t, q2t)):
            sc = _dot(k, qh) + bias_ref[0, e]
            s_ref[:, half * tq:(half + 1) * tq] = sc
            x_ref[:, half * tq:(half + 1) * tq] = jnp.max(sc, axis=0, keepdims=True)

    def softmax(s_ref, x_ref, p_ref, a_ref):
        m_old = m_scr[...]
        m_new = jnp.maximum(m_old, x_ref[...])
        a_ref[...] = jnp.exp2(m_old - m_new)
        m_scr[...] = m_new
        p_ref[...] = jnp.exp2(s_ref[...] - m_new).astype(BF16)

    def accumulate(kb, p_ref, a_ref):
        k0 = pl.multiple_of(kb * tk, tk)
        vx = jnp.concatenate([vt_ref[0, :, pl.ds(k0, tk)], ones_rows], axis=0)
        acc_scr[...] = a_ref[...] * acc_scr[...] + _dot(vx, p_ref[...])

    scores(0, s0, x0)
    scores(1, s1, x1)
    softmax(s0, x0, p0, a0)

    def steady(j, carry):
        t = 2 * j + 1
        scores(t + 1, s0, x0)
        softmax(s1, x1, p1, a1)
        accumulate(t - 1, p0, a0)
        scores(t + 2, s1, x1)
        softmax(s0, x0, p0, a0)
        accumulate(t, p1, a1)
        return carry

    lax.fori_loop(0, (nk - 2) // 2, steady, 0)
    softmax(s1, x1, p1, a1)
    accumulate(nk - 2, p0, a0)
    accumulate(nk - 1, p1, a1)

    acc = acc_scr[...]
    o = acc[:PAIR] * (1.0 / acc[PAIR:PAIR + 1])
    lam = lam_ref[...]
    lam_full = (jnp.exp(jnp.sum(lam[0:1] * lam[1:2], axis=1, keepdims=True))
                - jnp.exp(jnp.sum(lam[2:3] * lam[3:4], axis=1, keepdims=True)) + lam_init)
    o = o[:, :tq] - lam_full * o[:, tq:]
    inv = lax.rsqrt(jnp.mean(o * o, axis=0, keepdims=True) + EPS)
    y = o * inv * g_ref[...] * (1.0 - lam_init)
    o_ref[0] = y.T.astype(BF16)


def _diff_attention(qt, k, vt, lam, g, bias_tiles, layer_idx):
    bsz, s, _ = k.shape
    tq, tk = min(DIFF_TQ, s), min(DIFF_TK, s)
    assert (s // tk) % 2 == 0, "the key-block pipeline is unrolled by two"
    unit, e_lo, e_hi = _diff_tile_range(tq, tk)
    n_tiles = e_hi - e_lo + 1
    lam_init = 0.8 - 0.6 * math.exp(-0.3 * layer_idx)
    acc_rows = PAIR + DIFF_ONES_ROWS
    return pl.pallas_call(
        functools.partial(_diff_kernel, tk=tk, unit=unit, e_lo=e_lo, e_hi=e_hi, lam_init=lam_init),
        grid=(bsz, B_HEADS, s // tq),
        in_specs=[_const_spec(lam.shape),
                  pl.BlockSpec((1, PAIR, tq), lambda b, h, i: (b, h, i)),
                  pl.BlockSpec((1, s, PAIR), lambda b, h, i: (b, 0, h)),
                  pl.BlockSpec((1, PAIR, s), lambda b, h, i: (b, h, 0)),
                  pl.BlockSpec((1, n_tiles, tk, tq), lambda b, h, i: (h, 0, 0, 0),
                               pipeline_mode=pl.Buffered(1)),
                  pl.BlockSpec((PAIR, 1), lambda b, h, i: (h, 0))],
        out_specs=pl.BlockSpec((1, tq, PAIR), lambda b, h, i: (b, i, h)),
        out_shape=jax.ShapeDtypeStruct((bsz, s, B_WIDTH), BF16),
        scratch_shapes=[pltpu.VMEM((tk, 2 * tq), F32), pltpu.VMEM((tk, 2 * tq), F32),
                        pltpu.VMEM((tk, 2 * tq), BF16), pltpu.VMEM((tk, 2 * tq), BF16),
                        pltpu.VMEM((1, 2 * tq), F32), pltpu.VMEM((1, 2 * tq), F32),
                        pltpu.VMEM((1, 2 * tq), F32), pltpu.VMEM((1, 2 * tq), F32),
                        pltpu.VMEM((1, 2 * tq), F32), pltpu.VMEM((acc_rows, 2 * tq), F32)],
        compiler_params=_cparams("parallel", "parallel", "arbitrary"),
        name="diff_attention",
    )(lam, qt, k, vt, bias_tiles, g)


def _diff_bias_tiles(table, s):
    tq, tk = min(DIFF_TQ, s), min(DIFF_TK, s)
    unit, e_lo, e_hi = _diff_tile_range(tq, tk)
    return _bias_tiles(table, n_heads=B_HEADS, n_tiles=e_hi - e_lo + 1, tq=tq, tk=tk, col0=B_COL0,
                       offset0=e_lo * unit, stride=unit, dil=1, band=None, scale=LOG2E,
                       keys_on_rows=True)


def _local_kernel(*refs, hw, ti, k_blocks, n_res, with_sink, with_lse):
    refs = list(refs)
    sink_ref = refs.pop(0) if with_sink else None
    (q_ref, kp_ref, kc_ref, kn_ref, vp_ref, vc_ref, vn_ref, bias_ref) = refs[:8]
    o_ref = refs[8]
    lse_ref = refs[9] if with_lse else None
    kext, vext = refs[-2:]
    tb = q_ref.shape[1]
    i = pl.program_id(1)
    res = pl.program_id(2)
    seq = pl.num_programs(1) * tb
    kext[0:hw, :] = kp_ref[0]
    kext[hw:hw + tb, :] = kc_ref[0]
    kext[hw + tb:, :] = kn_ref[0]
    vext[0:hw, :] = vp_ref[0]
    vext[hw:hw + tb, :] = vc_ref[0]
    vext[hw + tb:, :] = vn_ref[0]
    wk = ti + 2 * hw
    n_sub = tb // ti
    low = lax.broadcasted_iota(jnp.int32, (ti, PAIR), 1) < HEAD_DIM
    kj = lax.broadcasted_iota(jnp.int32, (1, wk), 1)
    for j in range(n_sub):
        rows = slice(j * ti, (j + 1) * ti)
        krows = slice(j * ti, j * ti + wk)
        out_rows = rows if n_res == 1 else pl.ds(j * ti * n_res + res, ti, stride=n_res)
        edge = None
        if j in (0, n_sub - 1):
            kpos = i * tb + j * ti - hw + kj
            edge = jnp.where((kpos >= 0) & (kpos < seq), 0.0, NEG)
        for kb in sorted(set(k_blocks)):
            pairs = [p for p, b in enumerate(k_blocks) if b == kb]
            kk = kext[krows, kb * PAIR:(kb + 1) * PAIR]
            vv = vext[krows, kb * PAIR:(kb + 1) * PAIR]
            stack = []
            for p in pairs:
                q2 = q_ref[0, rows, p * PAIR:(p + 1) * PAIR]
                zero = jnp.zeros_like(q2)
                stack += [jnp.where(low, q2, zero), jnp.where(low, zero, q2)]
            sc_all = _dot_nt(jnp.concatenate(stack, axis=0), kk)
            probs, inv_dens, lses = [], [], []
            for n, p in enumerate(pairs):
                for half in range(2):
                    hd = 2 * p + half
                    sc = sc_all[(2 * n + half) * ti:(2 * n + half + 1) * ti] + bias_ref[hd]
                    if edge is not None:
                        sc = sc + edge
                    m = jnp.max(sc, axis=1, keepdims=True)
                    if with_sink:
                        m = jnp.maximum(m, sink_ref[hd])
                    pr = jnp.exp(sc - m)
                    den = jnp.sum(pr, axis=1, keepdims=True)
                    if with_sink:
                        den = den + jnp.exp(sink_ref[hd] - m)
                    probs.append(pr.astype(BF16))
                    inv_dens.append(1.0 / den)
                    lses.append(m + jnp.log(den))
            o_all = _dot(jnp.concatenate(probs, axis=0), vv)
            for n, p in enumerate(pairs):
                o0 = o_all[2 * n * ti:(2 * n + 1) * ti] * inv_dens[2 * n]
                o1 = o_all[(2 * n + 1) * ti:(2 * n + 2) * ti] * inv_dens[2 * n + 1]
                o_ref[0, p, out_rows, :] = jnp.where(low, o0, o1).astype(o_ref.dtype)
                if with_lse:
                    lse_ref[0, p, out_rows, :] = jnp.where(low, lses[2 * n], lses[2 * n + 1])


def _local_tiling(seq, n_res):
    tb = min(LOCAL_OUTER, seq, LOCAL_OUT_ROWS // n_res)
    return tb, min(LOCAL_TILE, tb)


def _local_attention(q_arr, k_arr, v_arr, bias, *, seq, n_res, q_width, q_col, k_col, v_col, hw,
                     k_blocks, out_dtype, sink=None, with_lse=False):
    bsz = q_arr.shape[0]
    tb, ti = _local_tiling(seq, n_res)
    kw = len(set(k_blocks)) * PAIR
    hb = tb // hw
    n_halo = seq // hw
    cur = lambda width, colf: pl.BlockSpec((1, tb, width), lambda b, i, r: (b, i, colf(r)))
    prev = lambda colf: pl.BlockSpec((1, hw, kw), lambda b, i, r: (b, jnp.maximum(i * hb - 1, 0), colf(r)))
    nxt = lambda colf: pl.BlockSpec(
        (1, hw, kw), lambda b, i, r: (b, jnp.minimum((i + 1) * hb, n_halo - 1), colf(r)))
    in_specs = [cur(q_width, q_col), prev(k_col), cur(kw, k_col), nxt(k_col),
                prev(v_col), cur(kw, v_col), nxt(v_col), _const_spec(bias.shape)]
    args = [q_arr, k_arr, k_arr, k_arr, v_arr, v_arr, v_arr, bias]
    if sink is not None:
        in_specs.insert(0, pl.BlockSpec(memory_space=pltpu.SMEM))
        args.insert(0, sink)
    n_pairs = q_width // PAIR
    out_spec = pl.BlockSpec((1, n_pairs, tb * n_res, PAIR), lambda b, i, r: (b, 0, i, 0))
    out_specs = [out_spec]
    out_shape = [jax.ShapeDtypeStruct((bsz, n_pairs, seq * n_res, PAIR), out_dtype)]
    if with_lse:
        out_specs.append(out_spec)
        out_shape.append(jax.ShapeDtypeStruct((bsz, n_pairs, seq * n_res, PAIR), F32))
    return pl.pallas_call(
        functools.partial(_local_kernel, hw=hw, ti=ti, k_blocks=tuple(k_blocks), n_res=n_res,
                          with_sink=sink is not None, with_lse=with_lse),
        grid=(bsz, seq // tb, n_res),
        in_specs=in_specs,
        out_specs=out_specs,
        out_shape=out_shape,
        scratch_shapes=[pltpu.VMEM((tb + 2 * hw, kw), BF16), pltpu.VMEM((tb + 2 * hw, kw), BF16)],
        compiler_params=_cparams("parallel", "parallel", "arbitrary"),
        name="local_attention",
    )(*args)


def _post_kernel(*refs, even):
    refs = list(refs)
    if even:
        a_ref, b_ref = refs[:2]
        rest = refs[2:]
    else:
        c_refs = refs[:6]
        d_ref = refs[6]
        rest = refs[7:]
    x_ref, mod_ref, gains_ref, wo_ref, w1_ref, w2_ref, out_ref = rest
    if even:
        y = _dot(a_ref[0], wo_ref[0:A_WIDTH, :]) + _dot(b_ref[0], wo_ref[A_WIDTH:, :])
    else:
        scaled = [[], [], []]
        for p in range(C_GW // PAIR):
            ls_ = [c_refs[2 * g + 1][0, p] for g in range(3)]
            mx = jnp.maximum(jnp.maximum(ls_[0], ls_[1]), ls_[2])
            es = [jnp.exp(l - mx) for l in ls_]
            inv = 1.0 / (es[0] + es[1] + es[2])
            for g in range(3):
                scaled[g].append((c_refs[2 * g][0, p] * (es[g] * inv)).astype(BF16))
        y = _dot(jnp.concatenate([d_ref[0, p] for p in range(D_WIDTH // PAIR)], axis=1),
                 wo_ref[C_WIDTH:, :])
        for g in range(3):
            y = y + _dot(jnp.concatenate(scaled[g], axis=1), wo_ref[g * C_GW:(g + 1) * C_GW, :])
    gate1 = mod_ref[0, 2:3, :]
    shift2 = mod_ref[0, 3:4, :]
    scale2 = mod_ref[0, 4:5, :]
    gate2 = mod_ref[0, 5:6, :]
    x = x_ref[0] + gate1 * _rms(y, gains_ref[0:1, :])
    h = _prenorm(x, gains_ref[1:2, :], shift2, scale2)
    acc = jnp.zeros_like(x)
    step = D_MODEL
    for j in range(D_FF // step):
        u = jnp.maximum(_dot(h, w1_ref[:, j * step:(j + 1) * step]), 0.0)
        acc = acc + _dot((u * u).astype(BF16), w2_ref[j * step:(j + 1) * step, :])
    out_ref[0] = x + gate2 * _rms(acc, gains_ref[2:3, :])


def _post_mixer(mix, x, mod, gains, wo, w1, w2, *, even):
    bsz, s, d = x.shape
    tm = min(TOKEN_TILE, s)
    tok = lambda width: pl.BlockSpec((1, tm, width), lambda b, i: (b, i, 0))
    resident = lambda a: pl.BlockSpec(a.shape, lambda b, i: (0, 0), pipeline_mode=pl.Buffered(1))
    slabs = lambda n: pl.BlockSpec((1, n, tm, PAIR), lambda b, i: (b, 0, i, 0))
    in_specs = [tok(m.shape[-1]) if m.ndim == 3 else slabs(m.shape[1]) for m in mix]
    in_specs += [tok(d), pl.BlockSpec((1, 6, d), lambda b, i: (b, 0, 0)), _const_spec(gains.shape),
                 resident(wo), resident(w1), resident(w2)]
    return pl.pallas_call(
        functools.partial(_post_kernel, even=even),
        grid=(bsz, s // tm),
        in_specs=in_specs,
        out_specs=tok(d),
        out_shape=jax.ShapeDtypeStruct((bsz, s, d), F32),
        compiler_params=_cparams("parallel", "parallel"),
        name="post_mixer_mlp",
    )(*mix, x, mod, gains, wo, w1, w2)


def _even_layer(x, mod, layer_idx, p, diff_tiles):
    e = layer_idx // 2
    w_in = p["even_w_in"][e]
    o1 = 2 * A_WIDTH
    o2 = o1 + 2 * A_WIDTH
    o3 = o2 + N_GATES
    w = {"qk": w_in[:, :o1].astype(BF16), "vo": w_in[:, o1:o2].astype(BF16),
         "g": w_in[:, o2:o3].astype(BF16), "b": w_in[:, o3:].astype(BF16)}
    qa, ka, va, oa, gates, qb, kb, vb = _even_in_proj(
        x, mod, p["mix_pre_g"][layer_idx][None], w, p["even_conv_w"][e],
        p["even_conv_b"][e][None], p["mlstm_gate_b"][e].reshape(1, N_GATES))
    gates_t = jnp.swapaxes(gates, 1, 2)
    ka_t = jnp.swapaxes(ka, 1, 2)
    h_fwd = _mlstm(qa, ka_t, va, gates, gates_t, reverse=False)
    mix_a = _mlstm(qa, ka_t, va, gates, gates_t, reverse=True, h_fwd=h_fwd, o_gate=oa,
                   norm_g=p["mlstm_norm_g"][e][None])
    mix_b = _diff_attention(jnp.swapaxes(qb, 1, 2), kb, jnp.swapaxes(vb, 1, 2), p["diff_lambda"][e],
                            p["diff_norm_g"][e][:, None], diff_tiles, layer_idx)
    return [mix_a, mix_b], p["even_w_out"][e].astype(BF16)


def _odd_layer(x, mod, layer_idx, p, c_tiles, d_tiles):
    o = layer_idx // 2
    bsz, s, _ = x.shape
    w_in = p["odd_w_in"][o]
    cw = C_WIDTH
    wq, wk, wv = w_in[:, :cw] * QK_SCALE, w_in[:, cw:2 * cw], w_in[:, 2 * cw:3 * cw]
    wc = jnp.stack([jnp.concatenate([t[:, g * C_GW:(g + 1) * C_GW] for t in (wq, wk, wv)], axis=1)
                    for g in range(len(C_PAIRS))]).astype(BF16)
    d0 = 3 * cw
    wqd = w_in[:, d0:d0 + D_WIDTH] * QK_SCALE
    wkd = w_in[:, d0 + D_WIDTH:d0 + D_WIDTH + D_KV_HEADS * HEAD_DIM]
    wvd = w_in[:, d0 + D_WIDTH + D_KV_HEADS * HEAD_DIM:]
    dup = lambda t: jnp.concatenate([t[:, g * HEAD_DIM:(g + 1) * HEAD_DIM]
                                     for g in range(D_KV_HEADS) for _ in range(2)], axis=1)
    wd = jnp.concatenate([wqd, dup(wkd), dup(wvd)], axis=1).astype(BF16)
    c0, c1, c2, qd, kd, vd = _odd_in_proj(x, mod, p["mix_pre_g"][layer_idx][None], {"c": wc, "d": wd})
    mix = []
    for g, (qkv, (window, dil)) in enumerate(zip((c0, c1, c2), C_PAIRS)):
        sub = s // dil
        strided = qkv.reshape(bsz, sub, dil * 3 * C_GW)
        o_g, lse_g = _local_attention(
            strided, strided, strided, c_tiles[g], seq=sub, n_res=dil, q_width=C_GW,
            q_col=lambda r: 3 * r, k_col=lambda r: 3 * r + 1, v_col=lambda r: 3 * r + 2,
            hw=window // (2 * dil), k_blocks=(0, 1), with_lse=True,
            out_dtype=BF16 if dil == 1 else F32)
        mix += [o_g, lse_g]
    o_d = _local_attention(qd, kd, vd, d_tiles, seq=s, n_res=1, q_width=D_WIDTH,
                           q_col=lambda r: 0, k_col=lambda r: 0, v_col=lambda r: 0,
                           hw=D_HALF_WINDOW, k_blocks=(0, 0, 1, 1), out_dtype=BF16,
                           sink=p["sink_logit"][o])[0]
    mix.append(o_d)
    return mix, p["odd_w_out"][o].astype(BF16)


def _local_bias_tiles(table, seq, *, n_heads, col0, hw, dil):
    _, ti = _local_tiling(seq, dil)
    return _bias_tiles(table, n_heads=n_heads, n_tiles=1, tq=ti, tk=ti + 2 * hw, col0=col0,
                       offset0=-hw, stride=0, dil=dil, band=hw)[:, 0]


def _trunk(x, mods, p):
    bsz, s, _ = x.shape
    table = p["rel_bias_table"]
    diff_tiles = _diff_bias_tiles(table, s)
    c_tiles = [_local_bias_tiles(table, s // dil, n_heads=C_HPG, col0=C_COL0 + g * C_HPG,
                                 hw=window // (2 * dil), dil=dil)
               for g, (window, dil) in enumerate(C_PAIRS)]
    d_tiles = _local_bias_tiles(table, s, n_heads=D_HEADS, col0=D_COL0, hw=D_HALF_WINDOW, dil=1)
    for l in range(DEPTH):
        mod = mods[l].reshape(bsz, 6, D_MODEL)
        if l % 2 == 0:
            mix, wo = _even_layer(x, mod, l, p, diff_tiles)
        else:
            mix, wo = _odd_layer(x, mod, l, p, c_tiles, d_tiles)
        gains = jnp.stack([p["mix_post_g"][l], p["mlp_pre_g"][l], p["mlp_post_g"][l]])
        x = _post_mixer(mix, x, mod, gains, wo, p["mlp_w1"][l].astype(BF16),
                        p["mlp_w2"][l].astype(BF16), even=l % 2 == 0)
    return x


def kernel(x_prompt, x_sample, c_prompt, c_sample, rel_bias_table, ada_w, ada_b, mix_pre_g, mix_post_g, mlp_pre_g, mlp_post_g, mlp_w1, mlp_w2, even_w_in, even_conv_w, even_conv_b, mlstm_gate_b, mlstm_norm_g, diff_lambda, diff_norm_g, even_w_out, odd_w_in, odd_w_out, sink_logit):
    p = dict(rel_bias_table=rel_bias_table, mix_pre_g=mix_pre_g, mix_post_g=mix_post_g,
             mlp_pre_g=mlp_pre_g, mlp_post_g=mlp_post_g, mlp_w1=mlp_w1, mlp_w2=mlp_w2,
             even_w_in=even_w_in, even_conv_w=even_conv_w, even_conv_b=even_conv_b,
             mlstm_gate_b=mlstm_gate_b, mlstm_norm_g=mlstm_norm_g, diff_lambda=diff_lambda,
             diff_norm_g=diff_norm_g, even_w_out=even_w_out, odd_w_in=odd_w_in,
             odd_w_out=odd_w_out, sink_logit=sink_logit)
    n_prompt = c_prompt.shape[0]
    mods = _modulation(jnp.concatenate([c_prompt, c_sample], axis=0), ada_w, ada_b)
    y_prompt = _trunk(x_prompt, mods[:, :n_prompt], p)
    y_sample = _trunk(x_sample, mods[:, n_prompt:], p)
    return (y_prompt, y_sample)
```

```python
import functools
import math

import jax
import jax.numpy as jnp
from jax import lax
from jax.experimental import pallas as pl
from jax.experimental.pallas import tpu as pltpu

F32 = jnp.float32
BF16 = jnp.bfloat16

D_MODEL = 1024
DEPTH = 4
HEAD_DIM = 64
PAIR = 2 * HEAD_DIM
EPS = 1e-6
NEG = -1e30
QK_SCALE = HEAD_DIM ** -0.5
LOG2E = math.log2(math.e)
A_HEADS = D_MODEL // 128
A_WIDTH = A_HEADS * HEAD_DIM
A_CONV = 5
A_CHUNK = 128
A_CHUNKS_PER_STEP = 8
N_GATES = 4 * A_HEADS
B_HEADS = D_MODEL // 256
B_WIDTH = B_HEADS * 2 * HEAD_DIM
C_PAIRS = ((128, 1), (512, 4), (2048, 16))
C_HPG = D_MODEL // 256
C_GW = C_HPG * HEAD_DIM
C_WIDTH = len(C_PAIRS) * C_GW
D_HEADS = D_MODEL // 128
D_KV_HEADS = D_HEADS // 4
D_WIDTH = D_HEADS * HEAD_DIM
D_HALF_WINDOW = 128
D_FF = 4 * D_MODEL
N_BUCKETS = 32
MAX_DISTANCE = 1024
B_COL0 = 0
C_COL0 = B_HEADS
D_COL0 = B_HEADS + len(C_PAIRS) * C_HPG

VMEM_LIMIT = 56 * 1024 * 1024
CONV_HALO = 16
TOKEN_TILE = 512
DIFF_TQ = 1024
DIFF_TK = 512
DIFF_ONES_ROWS = 16
LOCAL_TILE = 128
LOCAL_OUTER = 512
LOCAL_OUT_ROWS = 4096


def _cparams(*sem):
    return pltpu.CompilerParams(dimension_semantics=sem, vmem_limit_bytes=VMEM_LIMIT)


def _dot(a, b):
    return jnp.dot(a, b, preferred_element_type=F32)


def _dot_nt(a, b):
    return lax.dot_general(a, b, (((1,), (1,)), ((), ())), preferred_element_type=F32)


def _dot_tn(a, b):
    return lax.dot_general(a, b, (((0,), (0,)), ((), ())), preferred_element_type=F32)


def _rms(x, g):
    return x * lax.rsqrt(jnp.mean(x * x, axis=-1, keepdims=True) + EPS) * g


def _const_spec(shape):
    zeros = (0,) * len(shape)
    return pl.BlockSpec(shape, lambda *_: zeros)


def _rel_bucket(rel):
    half = N_BUCKETS // 2
    max_exact = half // 2
    ret = jnp.where(rel > 0, half, 0)
    n = jnp.abs(rel)
    nf = jnp.maximum(n, 1).astype(F32)
    large = max_exact + (jnp.log(nf / max_exact) / math.log(MAX_DISTANCE / max_exact)
                         * (half - max_exact)).astype(jnp.int32)
    large = jnp.minimum(large, half - 1)
    return ret + jnp.where(n < max_exact, n, large)


def _bias_tile_kernel(table_ref, out_ref, *, col0, offset0, stride, dil, band, scale, keys_on_rows):
    h = pl.program_id(0)
    t = pl.program_id(1)
    shape = out_ref.shape[2:]
    qi = lax.broadcasted_iota(jnp.int32, shape, 1 if keys_on_rows else 0)
    ki = lax.broadcasted_iota(jnp.int32, shape, 0 if keys_on_rows else 1)
    delta = offset0 + t * stride + ki - qi
    bucket = _rel_bucket(delta * dil)
    val = jnp.zeros(shape, F32)
    for j in range(N_BUCKETS):
        val = jnp.where(bucket == j, table_ref[j, col0 + h], val)
    if scale != 1.0:
        val = val * scale
    if band is not None:
        val = jnp.where(jnp.abs(delta) <= band, val, NEG)
    out_ref[0, 0] = val


def _bias_tiles(table, *, n_heads, n_tiles, tq, tk, col0, offset0, stride, dil, band, scale=1.0,
                keys_on_rows=False):
    shape = (tk, tq) if keys_on_rows else (tq, tk)
    return pl.pallas_call(
        functools.partial(_bias_tile_kernel, col0=col0, offset0=offset0, stride=stride, dil=dil,
                          band=band, scale=scale, keys_on_rows=keys_on_rows),
        grid=(n_heads, n_tiles),
        in_specs=[pl.BlockSpec(memory_space=pltpu.SMEM)],
        out_specs=pl.BlockSpec((1, 1) + shape, lambda h, t: (h, t, 0, 0)),
        out_shape=jax.ShapeDtypeStruct((n_heads, n_tiles) + shape, F32),
        compiler_params=_cparams("parallel", "parallel"),
        name="rel_bias_tiles",
    )(table)


def _mod_kernel(c_ref, w_ref, b_ref, o_ref):
    c = c_ref[...]
    a = (c * jax.nn.sigmoid(c)).astype(BF16)
    o_ref[0] = _dot(a, w_ref[0].astype(BF16)) + b_ref[0]


def _modulation(c_all, ada_w, ada_b):
    n, d = c_all.shape
    depth, _, width = ada_w.shape
    tn = 1536
    return pl.pallas_call(
        _mod_kernel,
        grid=(depth, width // tn),
        in_specs=[_const_spec((n, d)),
                  pl.BlockSpec((1, d, tn), lambda l, j: (l, 0, j)),
                  pl.BlockSpec((1, 1, tn), lambda l, j: (l, 0, j))],
        out_specs=pl.BlockSpec((1, n, tn), lambda l, j: (l, 0, j)),
        out_shape=jax.ShapeDtypeStruct((depth, n, width), F32),
        compiler_params=_cparams("parallel", "parallel"),
        name="adaln_modulation",
    )(c_all, ada_w, ada_b.reshape(depth, 1, width))


def _prenorm(x, g, shift, scale):
    return (_rms(x, g) * (1.0 + scale) + shift).astype(BF16)


def _even_in_kernel(x_ref, xp_ref, xn_ref, mod_ref, g_ref, wqk_ref, wvo_ref, wg_ref, wb_ref,
                    cw_ref, cb_ref, gb_ref,
                    qa_ref, ka_ref, va_ref, oa_ref, gates_ref, qb_ref, kb_ref, vb_ref, hx_ref, ext_ref):
    i = pl.program_id(1)
    n = pl.num_programs(1)
    tm = x_ref.shape[1]
    g = g_ref[...]
    shift = mod_ref[0, 0:1, :]
    scale = mod_ref[0, 1:2, :]
    h = _prenorm(x_ref[0], g, shift, scale)
    hx_ref[0:CONV_HALO, :] = _prenorm(xp_ref[0], g, shift, scale)
    hx_ref[CONV_HALO:CONV_HALO + tm, :] = h
    hx_ref[CONV_HALO + tm:, :] = _prenorm(xn_ref[0], g, shift, scale)
    ext_ref[...] = _dot(hx_ref[...], wqk_ref[...])
    ext_ref[0:CONV_HALO, :] = ext_ref[0:CONV_HALO, :] * (i > 0).astype(F32)
    ext_ref[CONV_HALO + tm:, :] = ext_ref[CONV_HALO + tm:, :] * (i < n - 1).astype(F32)
    acc = jnp.broadcast_to(cb_ref[...], (tm, 2 * A_WIDTH))
    for j in range(A_CONV):
        start = CONV_HALO - A_CONV // 2 + j
        acc = acc + cw_ref[j:j + 1, :] * ext_ref[start:start + tm, :]
    qk = acc * jax.nn.sigmoid(acc)
    qa_ref[0] = qk[:, :A_WIDTH].astype(BF16)
    ka_ref[0] = (qk[:, A_WIDTH:] * QK_SCALE).astype(BF16)
    vo = _dot(h, wvo_ref[...])
    va_ref[0] = vo[:, :A_WIDTH].astype(BF16)
    oa_ref[0] = vo[:, A_WIDTH:].astype(BF16)
    gates_ref[0] = _dot(h, wg_ref[...]) + gb_ref[...]
    qkv = _dot(h, wb_ref[...])
    qb_ref[0] = (qkv[:, :B_WIDTH] * (QK_SCALE * LOG2E)).astype(BF16)
    kb_ref[0] = qkv[:, B_WIDTH:2 * B_WIDTH].astype(BF16)
    vb_ref[0] = qkv[:, 2 * B_WIDTH:].astype(BF16)


def _even_in_proj(x, mod, g, w, conv_w, conv_b, gate_b):
    bsz, s, d = x.shape
    tm = min(TOKEN_TILE, s)
    nt = s // tm
    hb = tm // CONV_HALO
    n_halo = s // CONV_HALO
    tok = lambda width: pl.BlockSpec((1, tm, width), lambda b, i: (b, i, 0))
    outs = [A_WIDTH, A_WIDTH, A_WIDTH, A_WIDTH, N_GATES, B_WIDTH, B_WIDTH, B_WIDTH]
    dts = [BF16, BF16, BF16, BF16, F32, BF16, BF16, BF16]
    return pl.pallas_call(
        _even_in_kernel,
        grid=(bsz, nt),
        in_specs=[tok(d),
                  pl.BlockSpec((1, CONV_HALO, d), lambda b, i: (b, jnp.maximum(i * hb - 1, 0), 0)),
                  pl.BlockSpec((1, CONV_HALO, d),
                               lambda b, i: (b, jnp.minimum((i + 1) * hb, n_halo - 1), 0)),
                  pl.BlockSpec((1, 6, d), lambda b, i: (b, 0, 0)),
                  _const_spec((1, d)),
                  _const_spec(w["qk"].shape), _const_spec(w["vo"].shape),
                  _const_spec(w["g"].shape), _const_spec(w["b"].shape),
                  _const_spec(conv_w.shape), _const_spec(conv_b.shape), _const_spec(gate_b.shape)],
        out_specs=[tok(width) for width in outs],
        out_shape=[jax.ShapeDtypeStruct((bsz, s, width), dt) for width, dt in zip(outs, dts)],
        scratch_shapes=[pltpu.VMEM((tm + 2 * CONV_HALO, d), BF16),
                        pltpu.VMEM((tm + 2 * CONV_HALO, 2 * A_WIDTH), F32)],
        compiler_params=_cparams("parallel", "parallel"),
        name="even_in_proj",
    )(x, x, x, mod, g, w["qk"], w["vo"], w["g"], w["b"], conv_w, conv_b, gate_b)


def _odd_in_kernel(x_ref, mod_ref, g_ref, wc_ref, wd_ref, c0_ref, c1_ref, c2_ref, qd_ref, kd_ref, vd_ref,
                   slab_ref):
    h = _prenorm(x_ref[0], g_ref[...], mod_ref[0, 0:1, :], mod_ref[0, 1:2, :])
    tm = x_ref.shape[1]
    n_slabs = slab_ref.shape[0]
    for (_, dil), ref, g in zip(C_PAIRS, (c0_ref, c1_ref, c2_ref), range(len(C_PAIRS))):
        y = _dot(h, wc_ref[g])
        if dil == 1:
            ref[0] = y.astype(BF16)
            continue
        for c in range(n_slabs):
            slab_ref[c] = y[:, c * 128:(c + 1) * 128]
        for r in range(dil):
            for c in range(n_slabs):
                col = (r * n_slabs + c) * 128
                ref[0, :, col:col + 128] = slab_ref[c, pl.ds(r, tm // dil, stride=dil), :].astype(BF16)
    d = _dot(h, wd_ref[...])
    qd_ref[0] = d[:, :D_WIDTH].astype(BF16)
    kd_ref[0] = d[:, D_WIDTH:D_WIDTH + 2 * PAIR].astype(BF16)
    vd_ref[0] = d[:, D_WIDTH + 2 * PAIR:].astype(BF16)


def _odd_in_proj(x, mod, g, w):
    bsz, s, d = x.shape
    tm = min(TOKEN_TILE, s)
    tok = lambda width, dil=1: pl.BlockSpec((1, tm // dil, dil * width), lambda b, i: (b, i, 0))
    sds = lambda width, dil=1: jax.ShapeDtypeStruct((bsz, s // dil, dil * width), BF16)
    c_width = 3 * C_GW
    d_outs = [D_WIDTH, 2 * PAIR, 2 * PAIR]
    return pl.pallas_call(
        _odd_in_kernel,
        grid=(bsz, s // tm),
        in_specs=[tok(d), pl.BlockSpec((1, 6, d), lambda b, i: (b, 0, 0)), _const_spec((1, d)),
                  _const_spec(w["c"].shape), _const_spec(w["d"].shape)],
        out_specs=[tok(c_width, dil) for _, dil in C_PAIRS] + [tok(width) for width in d_outs],
        out_shape=[sds(c_width, dil) for _, dil in C_PAIRS] + [sds(width) for width in d_outs],
        scratch_shapes=[pltpu.VMEM((c_width // 128, tm, 128), F32)],
        compiler_params=_cparams("parallel", "parallel"),
        name="odd_in_proj",
    )(x, mod, g, w["c"], w["d"])


def _log_sigmoid(x):
    return jnp.minimum(x, 0.0) - jnp.log1p(jnp.exp(-jnp.abs(x)))


def _mlstm_kernel(*refs, reverse, n_chunks):
    if reverse:
        (q_ref, kt_ref, v_ref, gc_ref, gr_ref, hf_ref, oa_ref, ng_ref, out_ref, c_scr, m_scr) = refs
    else:
        (q_ref, kt_ref, v_ref, gc_ref, gr_ref, out_ref, c_scr, m_scr) = refs
    L = A_CHUNK

    @pl.when(pl.program_id(1) == 0)
    def _():
        c_scr[...] = jnp.zeros_like(c_scr)
        m_scr[...] = jnp.zeros_like(m_scr)

    row = lax.broadcasted_iota(jnp.int32, (L, L), 0)
    col = lax.broadcasted_iota(jnp.int32, (L, L), 1)
    causal = (col >= row) if reverse else (col <= row)
    causal_t = (row >= col) if reverse else (row <= col)
    causal_f = causal.astype(F32)
    causal_tf = causal_t.astype(F32)
    logf_c = _log_sigmoid(gc_ref[0]) * LOG2E
    logf_r = _log_sigmoid(gr_ref[0]) * LOG2E
    i_col0 = 2 * A_HEADS if reverse else 0
    f_col0 = i_col0 + A_HEADS
    last = 0 if reverse else L - 1
    low = lax.broadcasted_iota(jnp.int32, (L, PAIR), 1) < HEAD_DIM
    one = jnp.ones((L, PAIR), BF16)

    for c in (range(n_chunks - 1, -1, -1) if reverse else range(n_chunks)):
        rows = slice(c * L, (c + 1) * L)
        cum_c = jnp.dot(causal_f, logf_c[rows], precision=lax.Precision.HIGHEST,
                        preferred_element_type=F32)
        cum_r = jnp.dot(logf_r[:, rows], causal_tf, precision=lax.Precision.HIGHEST,
                        preferred_element_type=F32)
        r_all = (gr_ref[0, i_col0:i_col0 + A_HEADS, rows] * LOG2E
                 - cum_r[f_col0:f_col0 + A_HEADS, :])
        r_max_all = jnp.max(r_all, axis=1, keepdims=True)
        b_tot_all = cum_r[f_col0:f_col0 + A_HEADS, last:last + 1]
        m_prev_rows = m_scr[...]
        m_prev_all = m_prev_rows[:, 0:1]
        m_new_all = b_tot_all + jnp.maximum(m_prev_all, r_max_all)
        m_scr[...] = jnp.broadcast_to(m_new_all, m_scr.shape)
        decay_all = jnp.exp2(b_tot_all + m_prev_all - m_new_all)
        gain_all = jnp.exp2(b_tot_all + r_max_all - m_new_all)
        rp_all = jnp.maximum(r_all, m_prev_rows)
        for p in range(A_HEADS // 2):
            cols = slice(p * PAIR, (p + 1) * PAIR)
            q2 = q_ref[0, rows, cols]
            v2 = v_ref[0, rows, cols]
            kt2 = kt_ref[0, cols, rows]
            kt2f = kt2.astype(F32)
            tots, floors = [], []
            for half in range(2):
                hd = 2 * p + half
                mine = low if half == 0 else jnp.logical_not(low)
                qm = jnp.where(mine, q2, jnp.zeros_like(q2))
                vx = jnp.where(mine, v2, one)
                r_r = r_all[hd:hd + 1, :]
                b_c = cum_c[:, f_col0 + hd:f_col0 + hd + 1]
                c_ext = c_scr[hd]
                a = jnp.max(jnp.where(causal, rp_all[hd:hd + 1, :], -jnp.inf), axis=1, keepdims=True)
                sc = _dot(qm, kt2) * jnp.exp2(jnp.where(causal, r_r, -jnp.inf) - a)
                tots.append(_dot(sc.astype(BF16), vx)
                            + jnp.exp2(m_prev_rows[hd:hd + 1, :] - a) * _dot(qm, c_ext.astype(BF16)))
                floors.append(jnp.exp2(-(b_c + a)))
                upd = _dot((kt2f * jnp.exp2(r_r - r_max_all[hd:hd + 1, :])).astype(BF16), vx)
                c_scr[hd] = decay_all[hd:hd + 1, :] * c_ext + gain_all[hd:hd + 1, :] * upd
            den = pltpu.roll(jnp.where(low, tots[1], tots[0]), HEAD_DIM, axis=1)
            floor = jnp.where(low, floors[0], floors[1])
            h2 = jnp.where(low, tots[0], tots[1]) * (1.0 / jnp.maximum(jnp.abs(den), floor))
            if not reverse:
                out_ref[0, rows, cols] = h2
            else:
                hs = h2 + hf_ref[0, rows, cols]
                sq = hs * hs
                ms0 = jnp.sum(jnp.where(low, sq, 0.0), axis=1, keepdims=True) / HEAD_DIM
                ms1 = jnp.sum(jnp.where(low, 0.0, sq), axis=1, keepdims=True) / HEAD_DIM
                rs = jnp.where(low, lax.rsqrt(ms0 + EPS), lax.rsqrt(ms1 + EPS))
                y = hs * rs * ng_ref[:, cols]
                out_ref[0, rows, cols] = (jax.nn.sigmoid(oa_ref[0, rows, cols].astype(F32)) * y
                                          ).astype(BF16)


def _mlstm(q, kt, v, gates, gates_t, *, reverse, h_fwd=None, o_gate=None, norm_g=None):
    bsz, s, _ = q.shape
    n_chunks = min(A_CHUNKS_PER_STEP, s // A_CHUNK)
    tl = n_chunks * A_CHUNK
    nb = s // tl
    pos = (lambda c: nb - 1 - c) if reverse else (lambda c: c)
    tok = lambda width: pl.BlockSpec((1, tl, width), lambda b, c: (b, pos(c), 0))
    tok_t = lambda width: pl.BlockSpec((1, width, tl), lambda b, c: (b, 0, pos(c)))
    in_specs = [tok(A_WIDTH), tok_t(A_WIDTH), tok(A_WIDTH), tok(N_GATES), tok_t(N_GATES)]
    args = [q, kt, v, gates, gates_t]
    if reverse:
        in_specs += [tok(A_WIDTH), tok(A_WIDTH), _const_spec((1, A_WIDTH))]
        args += [h_fwd, o_gate, norm_g]
    return pl.pallas_call(
        functools.partial(_mlstm_kernel, reverse=reverse, n_chunks=n_chunks),
        grid=(bsz, nb),
        in_specs=in_specs,
        out_specs=tok(A_WIDTH),
        out_shape=jax.ShapeDtypeStruct((bsz, s, A_WIDTH), BF16 if reverse else F32),
        scratch_shapes=[pltpu.VMEM((A_HEADS, PAIR, PAIR), F32), pltpu.VMEM((A_HEADS, 128), F32)],
        compiler_params=_cparams("parallel", "arbitrary"),
        name="mlstm_bwd" if reverse else "mlstm_fwd",
    )(*args)


def _diff_tile_range(tq, tk):
    unit = math.gcd(tq, tk)
    e_hi = pl.cdiv(MAX_DISTANCE + tq - 1, unit)
    e_lo = -pl.cdiv(MAX_DISTANCE + tk - 1, unit)
    return unit, e_lo, e_hi


def _diff_kernel(lam_ref, qt_ref, k_ref, vt_ref, bias_ref, g_ref, o_ref,
                 s0, s1, p0, p1, a0, a1, x0, x1, m_scr, acc_scr, *, tk, unit, e_lo, e_hi, lam_init):
    tq = qt_ref.shape[2]
    nk = k_ref.shape[1] // tk
    q0 = pl.program_id(2) * tq
    qt = qt_ref[0]
    first = lax.broadcasted_iota(jnp.int32, qt.shape, 0) < HEAD_DIM
    zero = jnp.zeros_like(qt)
    q1t = jnp.where(first, qt, zero)
    q2t = jnp.where(first, zero, qt)
    ones_rows = jnp.ones((DIFF_ONES_ROWS, tk), BF16)
    m_scr[...] = jnp.full(m_scr.shape, -jnp.inf, F32)
    acc_scr[...] = jnp.zeros(acc_scr.shape, F32)

    def scores(kb, s_ref, x_ref):
        k0 = pl.multiple_of(kb * tk, tk)
        e = jnp.clip((k0 - q0) // unit, e_lo, e_hi) - e_lo
        k = k_ref[0, pl.ds(k0, tk), :]
        for half, qh in enumerate((q1t, q2t)):
            sc = _dot(k, qh) + bias_ref[0, e]
            s_ref[:, half * tq:(half + 1) * tq] = sc
            x_ref[:, half * tq:(half + 1) * tq] = jnp.max(sc, axis=0, keepdims=True)

    def softmax(s_ref, x_ref, p_ref, a_ref):
        m_old = m_scr[...]
        m_new = jnp.maximum(m_old, x_ref[...])
        a_ref[...] = jnp.exp2(m_old - m_new)
        m_scr[...] = m_new
        p_ref[...] = jnp.exp2(s_ref[...] - m_new).astype(BF16)

    def accumulate(kb, p_ref, a_ref):
        k0 = pl.multiple_of(kb * tk, tk)
        vx = jnp.concatenate([vt_ref[0, :, pl.ds(k0, tk)], ones_rows], axis=0)
        acc_scr[...] = a_ref[...] * acc_scr[...] + _dot(vx, p_ref[...])

    scores(0, s0, x0)
    scores(1, s1, x1)
    softmax(s0, x0, p0, a0)

    def steady(j, carry):
        t = 2 * j + 1
        scores(t + 1, s0, x0)
        softmax(s1, x1, p1, a1)
        accumulate(t - 1, p0, a0)
        scores(t + 2, s1, x1)
        softmax(s0, x0, p0, a0)
        accumulate(t, p1, a1)
        return carry

    lax.fori_loop(0, (nk - 2) // 2, steady, 0)
    softmax(s1, x1, p1, a1)
    accumulate(nk - 2, p0, a0)
    accumulate(nk - 1, p1, a1)

    acc = acc_scr[...]
    o = acc[:PAIR] * (1.0 / acc[PAIR:PAIR + 1])
    lam = lam_ref[...]
    lam_full = (jnp.exp(jnp.sum(lam[0:1] * lam[1:2], axis=1, keepdims=True))
                - jnp.exp(jnp.sum(lam[2:3] * lam[3:4], axis=1, keepdims=True)) + lam_init)
    o = o[:, :tq] - lam_full * o[:, tq:]
    inv = lax.rsqrt(jnp.mean(o * o, axis=0, keepdims=True) + EPS)
    y = o * inv * g_ref[...] * (1.0 - lam_init)
    o_ref[0] = y.T.astype(BF16)


def _diff_attention(qt, k, vt, lam, g, bias_tiles, layer_idx):
    bsz, s, _ = k.shape
    tq, tk = min(DIFF_TQ, s), min(DIFF_TK, s)
    assert (s // tk) % 2 == 0, "the key-block pipeline is unrolled by two"
    unit, e_lo, e_hi = _diff_tile_range(tq, tk)
    n_tiles = e_hi - e_lo + 1
    lam_init = 0.8 - 0.6 * math.exp(-0.3 * layer_idx)
    acc_rows = PAIR + DIFF_ONES_ROWS
    return pl.pallas_call(
        functools.partial(_diff_kernel, tk=tk, unit=unit, e_lo=e_lo, e_hi=e_hi, lam_init=lam_init),
        grid=(bsz, B_HEADS, s // tq),
        in_specs=[_const_spec(lam.shape),
                  pl.BlockSpec((1, PAIR, tq), lambda b, h, i: (b, h, i)),
                  pl.BlockSpec((1, s, PAIR), lambda b, h, i: (b, 0, h)),
                  pl.BlockSpec((1, PAIR, s), lambda b, h, i: (b, h, 0)),
                  pl.BlockSpec((1, n_tiles, tk, tq), lambda b, h, i: (h, 0, 0, 0),
                               pipeline_mode=pl.Buffered(1)),
                  pl.BlockSpec((PAIR, 1), lambda b, h, i: (h, 0))],
        out_specs=pl.BlockSpec((1, tq, PAIR), lambda b, h, i: (b, i, h)),
        out_shape=jax.ShapeDtypeStruct((bsz, s, B_WIDTH), BF16),
        scratch_shapes=[pltpu.VMEM((tk, 2 * tq), F32), pltpu.VMEM((tk, 2 * tq), F32),
                        pltpu.VMEM((tk, 2 * tq), BF16), pltpu.VMEM((tk, 2 * tq), BF16),
                        pltpu.VMEM((1, 2 * tq), F32), pltpu.VMEM((1, 2 * tq), F32),
                        pltpu.VMEM((1, 2 * tq), F32), pltpu.VMEM((1, 2 * tq), F32),
                        pltpu.VMEM((1, 2 * tq), F32), pltpu.VMEM((acc_rows, 2 * tq), F32)],
        compiler_params=_cparams("parallel", "parallel", "arbitrary"),
        name="diff_attention",
    )(lam, qt, k, vt, bias_tiles, g)


def _diff_bias_tiles(table, s):
    tq, tk = min(DIFF_TQ, s), min(DIFF_TK, s)
    unit, e_lo, e_hi = _diff_tile_range(tq, tk)
    return _bias_tiles(table, n_heads=B_HEADS, n_tiles=e_hi - e_lo + 1, tq=tq, tk=tk, col0=B_COL0,
                       offset0=e_lo * unit, stride=unit, dil=1, band=None, scale=LOG2E,
                       keys_on_rows=True)


def _local_kernel(*refs, hw, ti, k_blocks, n_res, with_sink, with_lse):
    refs = list(refs)
    sink_ref = refs.pop(0) if with_sink else None
    (q_ref, kp_ref, kc_ref, kn_ref, vp_ref, vc_ref, vn_ref, bias_ref) = refs[:8]
    o_ref = refs[8]
    lse_ref = refs[9] if with_lse else None
    kext, vext = refs[-2:]
    tb = q_ref.shape[1]
    i = pl.program_id(1)
    res = pl.program_id(2)
    seq = pl.num_programs(1) * tb
    kext[0:hw, :] = kp_ref[0]
    kext[hw:hw + tb, :] = kc_ref[0]
    kext[hw + tb:, :] = kn_ref[0]
    vext[0:hw, :] = vp_ref[0]
    vext[hw:hw + tb, :] = vc_ref[0]
    vext[hw + tb:, :] = vn_ref[0]
    wk = ti + 2 * hw
    n_sub = tb // ti
    low = lax.broadcasted_iota(jnp.int32, (ti, PAIR), 1) < HEAD_DIM
    kj = lax.broadcasted_iota(jnp.int32, (1, wk), 1)
    for j in range(n_sub):
        rows = slice(j * ti, (j + 1) * ti)
        krows = slice(j * ti, j * ti + wk)
        out_rows = rows if n_res == 1 else pl.ds(j * ti * n_res + res, ti, stride=n_res)
        edge = None
        if j in (0, n_sub - 1):
            kpos = i * tb + j * ti - hw + kj
            edge = jnp.where((kpos >= 0) & (kpos < seq), 0.0, NEG)
        for kb in sorted(set(k_blocks)):
            pairs = [p for p, b in enumerate(k_blocks) if b == kb]
            kk = kext[krows, kb * PAIR:(kb + 1) * PAIR]
            vv = vext[krows, kb * PAIR:(kb + 1) * PAIR]
            stack = []
            for p in pairs:
                q2 = q_ref[0, rows, p * PAIR:(p + 1) * PAIR]
                zero = jnp.zeros_like(q2)
                stack += [jnp.where(low, q2, zero), jnp.where(low, zero, q2)]
            sc_all = _dot_nt(jnp.concatenate(stack, axis=0), kk)
            probs, inv_dens, lses = [], [], []
            for n, p in enumerate(pairs):
                for half in range(2):
                    hd = 2 * p + half
                    sc = sc_all[(2 * n + half) * ti:(2 * n + half + 1) * ti] + bias_ref[hd]
                    if edge is not None:
                        sc = sc + edge
                    m = jnp.max(sc, axis=1, keepdims=True)
                    if with_sink:
                        m = jnp.maximum(m, sink_ref[hd])
                    pr = jnp.exp(sc - m)
                    den = jnp.sum(pr, axis=1, keepdims=True)
                    if with_sink:
                        den = den + jnp.exp(sink_ref[hd] - m)
                    probs.append(pr.astype(BF16))
                    inv_dens.append(1.0 / den)
                    lses.append(m + jnp.log(den))
            o_all = _dot(jnp.concatenate(probs, axis=0), vv)
            for n, p in enumerate(pairs):
                o0 = o_all[2 * n * ti:(2 * n + 1) * ti] * inv_dens[2 * n]
                o1 = o_all[(2 * n + 1) * ti:(2 * n + 2) * ti] * inv_dens[2 * n + 1]
                o_ref[0, p, out_rows, :] = jnp.where(low, o0, o1).astype(o_ref.dtype)
                if with_lse:
                    lse_ref[0, p, out_rows, :] = jnp.where(low, lses[2 * n], lses[2 * n + 1])


def _local_tiling(seq, n_res):
    tb = min(LOCAL_OUTER, seq, LOCAL_OUT_ROWS // n_res)
    return tb, min(LOCAL_TILE, tb)


def _local_attention(q_arr, k_arr, v_arr, bias, *, seq, n_res, q_width, q_col, k_col, v_col, hw,
                     k_blocks, out_dtype, sink=None, with_lse=False):
    bsz = q_arr.shape[0]
    tb, ti = _local_tiling(seq, n_res)
    kw = len(set(k_blocks)) * PAIR
    hb = tb // hw
    n_halo = seq // hw
    cur = lambda width, colf: pl.BlockSpec((1, tb, width), lambda b, i, r: (b, i, colf(r)))
    prev = lambda colf: pl.BlockSpec((1, hw, kw), lambda b, i, r: (b, jnp.maximum(i * hb - 1, 0), colf(r)))
    nxt = lambda colf: pl.BlockSpec(
        (1, hw, kw), lambda b, i, r: (b, jnp.minimum((i + 1) * hb, n_halo - 1), colf(r)))
    in_specs = [cur(q_width, q_col), prev(k_col), cur(kw, k_col), nxt(k_col),
                prev(v_col), cur(kw, v_col), nxt(v_col), _const_spec(bias.shape)]
    args = [q_arr, k_arr, k_arr, k_arr, v_arr, v_arr, v_arr, bias]
    if sink is not None:
        in_specs.insert(0, pl.BlockSpec(memory_space=pltpu.SMEM))
        args.insert(0, sink)
    n_pairs = q_width // PAIR
    out_spec = pl.BlockSpec((1, n_pairs, tb * n_res, PAIR), lambda b, i, r: (b, 0, i, 0))
    out_specs = [out_spec]
    out_shape = [jax.ShapeDtypeStruct((bsz, n_pairs, seq * n_res, PAIR), out_dtype)]
    if with_lse:
        out_specs.append(out_spec)
        out_shape.append(jax.ShapeDtypeStruct((bsz, n_pairs, seq * n_res, PAIR), F32))
    return pl.pallas_call(
        functools.partial(_local_kernel, hw=hw, ti=ti, k_blocks=tuple(k_blocks), n_res=n_res,
                          with_sink=sink is not None, with_lse=with_lse),
        grid=(bsz, seq // tb, n_res),
        in_specs=in_specs,
        out_specs=out_specs,
        out_shape=out_shape,
        scratch_shapes=[pltpu.VMEM((tb + 2 * hw, kw), BF16), pltpu.VMEM((tb + 2 * hw, kw), BF16)],
        compiler_params=_cparams("parallel", "parallel", "arbitrary"),
        name="local_attention",
    )(*args)


def _post_kernel(*refs, even):
    refs = list(refs)
    if even:
        a_ref, b_ref = refs[:2]
        rest = refs[2:]
    else:
        c_refs = refs[:6]
        d_ref = refs[6]
        rest = refs[7:]
    x_ref, mod_ref, gains_ref, wo_ref, w1_ref, w2_ref, out_ref = rest
    if even:
        y = _dot(a_ref[0], wo_ref[0:A_WIDTH, :]) + _dot(b_ref[0], wo_ref[A_WIDTH:, :])
    else:
        scaled = [[], [], []]
        for p in range(C_GW // PAIR):
            ls_ = [c_refs[2 * g + 1][0, p] for g in range(3)]
            mx = jnp.maximum(jnp.maximum(ls_[0], ls_[1]), ls_[2])
            es = [jnp.exp(l - mx) for l in ls_]
            inv = 1.0 / (es[0] + es[1] + es[2])
            for g in range(3):
                scaled[g].append((c_refs[2 * g][0, p] * (es[g] * inv)).astype(BF16))
        y = _dot(jnp.concatenate([d_ref[0, p] for p in range(D_WIDTH // PAIR)], axis=1),
                 wo_ref[C_WIDTH:, :])
        for g in range(3):
            y = y + _dot(jnp.concatenate(scaled[g], axis=1), wo_ref[g * C_GW:(g + 1) * C_GW, :])
    gate1 = mod_ref[0, 2:3, :]
    shift2 = mod_ref[0, 3:4, :]
    scale2 = mod_ref[0, 4:5, :]
    gate2 = mod_ref[0, 5:6, :]
    x = x_ref[0] + gate1 * _rms(y, gains_ref[0:1, :])
    h = _prenorm(x, gains_ref[1:2, :], shift2, scale2)
    acc = jnp.zeros_like(x)
    step = D_MODEL
    for j in range(D_FF // step):
        u = jnp.maximum(_dot(h, w1_ref[:, j * step:(j + 1) * step]), 0.0)
        acc = acc + _dot((u * u).astype(BF16), w2_ref[j * step:(j + 1) * step, :])
    out_ref[0] = x + gate2 * _rms(acc, gains_ref[2:3, :])


def _post_mixer(mix, x, mod, gains, wo, w1, w2, *, even):
    bsz, s, d = x.shape
    tm = min(TOKEN_TILE, s)
    tok = lambda width: pl.BlockSpec((1, tm, width), lambda b, i: (b, i, 0))
    resident = lambda a: pl.BlockSpec(a.shape, lambda b, i: (0, 0), pipeline_mode=pl.Buffered(1))
    slabs = lambda n: pl.BlockSpec((1, n, tm, PAIR), lambda b, i: (b, 0, i, 0))
    in_specs = [tok(m.shape[-1]) if m.ndim == 3 else slabs(m.shape[1]) for m in mix]
    in_specs += [tok(d), pl.BlockSpec((1, 6, d), lambda b, i: (b, 0, 0)), _const_spec(gains.shape),
                 resident(wo), resident(w1), resident(w2)]
    return pl.pallas_call(
        functools.partial(_post_kernel, even=even),
        grid=(bsz, s // tm),
        in_specs=in_specs,
        out_specs=tok(d),
        out_shape=jax.ShapeDtypeStruct((bsz, s, d), F32),
        compiler_params=_cparams("parallel", "parallel"),
        name="post_mixer_mlp",
    )(*mix, x, mod, gains, wo, w1, w2)


def _even_layer(x, mod, layer_idx, p, diff_tiles):
    e = layer_idx // 2
    w_in = p["even_w_in"][e]
    o1 = 2 * A_WIDTH
    o2 = o1 + 2 * A_WIDTH
    o3 = o2 + N_GATES
    w = {"qk": w_in[:, :o1].astype(BF16), "vo": w_in[:, o1:o2].astype(BF16),
         "g": w_in[:, o2:o3].astype(BF16), "b": w_in[:, o3:].astype(BF16)}
    qa, ka, va, oa, gates, qb, kb, vb = _even_in_proj(
        x, mod, p["mix_pre_g"][layer_idx][None], w, p["even_conv_w"][e],
        p["even_conv_b"][e][None], p["mlstm_gate_b"][e].reshape(1, N_GATES))
    gates_t = jnp.swapaxes(gates, 1, 2)
    ka_t = jnp.swapaxes(ka, 1, 2)
    h_fwd = _mlstm(qa, ka_t, va, gates, gates_t, reverse=False)
    mix_a = _mlstm(qa, ka_t, va, gates, gates_t, reverse=True, h_fwd=h_fwd, o_gate=oa,
                   norm_g=p["mlstm_norm_g"][e][None])
    mix_b = _diff_attention(jnp.swapaxes(qb, 1, 2), kb, jnp.swapaxes(vb, 1, 2), p["diff_lambda"][e],
                            p["diff_norm_g"][e][:, None], diff_tiles, layer_idx)
    return [mix_a, mix_b], p["even_w_out"][e].astype(BF16)


def _odd_layer(x, mod, layer_idx, p, c_tiles, d_tiles):
    o = layer_idx // 2
    bsz, s, _ = x.shape
    w_in = p["odd_w_in"][o]
    cw = C_WIDTH
    wq, wk, wv = w_in[:, :cw] * QK_SCALE, w_in[:, cw:2 * cw], w_in[:, 2 * cw:3 * cw]
    wc = jnp.stack([jnp.concatenate([t[:, g * C_GW:(g + 1) * C_GW] for t in (wq, wk, wv)], axis=1)
                    for g in range(len(C_PAIRS))]).astype(BF16)
    d0 = 3 * cw
    wqd = w_in[:, d0:d0 + D_WIDTH] * QK_SCALE
    wkd = w_in[:, d0 + D_WIDTH:d0 + D_WIDTH + D_KV_HEADS * HEAD_DIM]
    wvd = w_in[:, d0 + D_WIDTH + D_KV_HEADS * HEAD_DIM:]
    dup = lambda t: jnp.concatenate([t[:, g * HEAD_DIM:(g + 1) * HEAD_DIM]
                                     for g in range(D_KV_HEADS) for _ in range(2)], axis=1)
    wd = jnp.concatenate([wqd, dup(wkd), dup(wvd)], axis=1).astype(BF16)
    c0, c1, c2, qd, kd, vd = _odd_in_proj(x, mod, p["mix_pre_g"][layer_idx][None], {"c": wc, "d": wd})
    mix = []
    for g, (strided, (window, dil)) in enumerate(zip((c0, c1, c2), C_PAIRS)):
        sub = s // dil
        o_g, lse_g = _local_attention(
            strided, strided, strided, c_tiles[g], seq=sub, n_res=dil, q_width=C_GW,
            q_col=lambda r: 3 * r, k_col=lambda r: 3 * r + 1, v_col=lambda r: 3 * r + 2,
            hw=window // (2 * dil), k_blocks=(0, 1), with_lse=True,
            out_dtype=BF16 if dil == 1 else F32)
        mix += [o_g, lse_g]
    o_d = _local_attention(qd, kd, vd, d_tiles, seq=s, n_res=1, q_width=D_WIDTH,
                           q_col=lambda r: 0, k_col=lambda r: 0, v_col=lambda r: 0,
                           hw=D_HALF_WINDOW, k_blocks=(0, 0, 1, 1), out_dtype=BF16,
                           sink=p["sink_logit"][o])[0]
    mix.append(o_d)
    return mix, p["odd_w_out"][o].astype(BF16)


def _local_bias_tiles(table, seq, *, n_heads, col0, hw, dil):
    _, ti = _local_tiling(seq, dil)
    return _bias_tiles(table, n_heads=n_heads, n_tiles=1, tq=ti, tk=ti + 2 * hw, col0=col0,
                       offset0=-hw, stride=0, dil=dil, band=hw)[:, 0]


def _trunk(x, mods, p):
    bsz, s, _ = x.shape
    table = p["rel_bias_table"]
    diff_tiles = _diff_bias_tiles(table, s)
    c_tiles = [_local_bias_tiles(table, s // dil, n_heads=C_HPG, col0=C_COL0 + g * C_HPG,
                                 hw=window // (2 * dil), dil=dil)
               for g, (window, dil) in enumerate(C_PAIRS)]
    d_tiles = _local_bias_tiles(table, s, n_heads=D_HEADS, col0=D_COL0, hw=D_HALF_WINDOW, dil=1)
    for l in range(DEPTH):
        mod = mods[l].reshape(bsz, 6, D_MODEL)
        if l % 2 == 0:
            mix, wo = _even_layer(x, mod, l, p, diff_tiles)
        else:
            mix, wo = _odd_layer(x, mod, l, p, c_tiles, d_tiles)
        gains = jnp.stack([p["mix_post_g"][l], p["mlp_pre_g"][l], p["mlp_post_g"][l]])
        x = _post_mixer(mix, x, mod, gains, wo, p["mlp_w1"][l].astype(BF16),
                        p["mlp_w2"][l].astype(BF16), even=l % 2 == 0)
    return x


def kernel(x_prompt, x_sample, c_prompt, c_sample, rel_bias_table, ada_w, ada_b, mix_pre_g, mix_post_g, mlp_pre_g, mlp_post_g, mlp_w1, mlp_w2, even_w_in, even_conv_w, even_conv_b, mlstm_gate_b, mlstm_norm_g, diff_lambda, diff_norm_g, even_w_out, odd_w_in, odd_w_out, sink_logit):
    p = dict(rel_bias_table=rel_bias_table, mix_pre_g=mix_pre_g, mix_post_g=mix_post_g,
             mlp_pre_g=mlp_pre_g, mlp_post_g=mlp_post_g, mlp_w1=mlp_w1, mlp_w2=mlp_w2,
             even_w_in=even_w_in, even_conv_w=even_conv_w, even_conv_b=even_conv_b,
             mlstm_gate_b=mlstm_gate_b, mlstm_norm_g=mlstm_norm_g, diff_lambda=diff_lambda,
             diff_norm_g=diff_norm_g, even_w_out=even_w_out, odd_w_in=odd_w_in,
             odd_w_out=odd_w_out, sink_logit=sink_logit)
    n_prompt = c_prompt.shape[0]
    mods = _modulation(jnp.concatenate([c_prompt, c_sample], axis=0), ada_w, ada_b)
    y_prompt = _trunk(x_prompt, mods[:, :n_prompt], p)
    y_sample = _trunk(x_sample, mods[:, n_prompt:], p)
    return (y_prompt, y_sample)
```

```python
import functools
import math

import jax
import jax.numpy as jnp
from jax import lax
from jax.experimental import pallas as pl
from jax.experimental.pallas import tpu as pltpu

F32 = jnp.float32
BF16 = jnp.bfloat16

D_MODEL = 1024
DEPTH = 4
HEAD_DIM = 64
PAIR = 2 * HEAD_DIM
EPS = 1e-6
NEG = -1e30
QK_SCALE = HEAD_DIM ** -0.5
LOG2E = math.log2(math.e)
LN2 = math.log(2.0)
A_HEADS = D_MODEL // 128
A_WIDTH = A_HEADS * HEAD_DIM
A_CONV = 5
A_CHUNK = 128
A_CHUNKS_PER_STEP = 8
N_GATES = 4 * A_HEADS
B_HEADS = D_MODEL // 256
B_WIDTH = B_HEADS * 2 * HEAD_DIM
C_PAIRS = ((128, 1), (512, 4), (2048, 16))
C_HPG = D_MODEL // 256
C_GW = C_HPG * HEAD_DIM
C_WIDTH = len(C_PAIRS) * C_GW
D_HEADS = D_MODEL // 128
D_KV_HEADS = D_HEADS // 4
D_WIDTH = D_HEADS * HEAD_DIM
D_HALF_WINDOW = 128
D_FF = 4 * D_MODEL
N_BUCKETS = 32
MAX_DISTANCE = 1024
B_COL0 = 0
C_COL0 = B_HEADS
D_COL0 = B_HEADS + len(C_PAIRS) * C_HPG

VMEM_LIMIT = 56 * 1024 * 1024
CONV_HALO = 16
TOKEN_TILE = 512
DIFF_TQ = 1024
DIFF_TK = 512
DIFF_ONES_ROWS = 16
LOCAL_TILE = 128
LOCAL_OUTER = 512
LOCAL_OUT_ROWS = 4096


def _cparams(*sem):
    return pltpu.CompilerParams(dimension_semantics=sem, vmem_limit_bytes=VMEM_LIMIT)


def _dot(a, b):
    return jnp.dot(a, b, preferred_element_type=F32)


def _dot_nt(a, b):
    return lax.dot_general(a, b, (((1,), (1,)), ((), ())), preferred_element_type=F32)


def _dot_tn(a, b):
    return lax.dot_general(a, b, (((0,), (0,)), ((), ())), preferred_element_type=F32)


def _rms(x, g):
    return x * lax.rsqrt(jnp.mean(x * x, axis=-1, keepdims=True) + EPS) * g


def _const_spec(shape):
    zeros = (0,) * len(shape)
    return pl.BlockSpec(shape, lambda *_: zeros)


def _rel_bucket(rel):
    half = N_BUCKETS // 2
    max_exact = half // 2
    ret = jnp.where(rel > 0, half, 0)
    n = jnp.abs(rel)
    nf = jnp.maximum(n, 1).astype(F32)
    large = max_exact + (jnp.log(nf / max_exact) / math.log(MAX_DISTANCE / max_exact)
                         * (half - max_exact)).astype(jnp.int32)
    large = jnp.minimum(large, half - 1)
    return ret + jnp.where(n < max_exact, n, large)


def _bias_tile_kernel(table_ref, out_ref, *, col0, offset0, stride, dil, band, scale, keys_on_rows):
    h = pl.program_id(0)
    t = pl.program_id(1)
    shape = out_ref.shape[2:]
    qi = lax.broadcasted_iota(jnp.int32, shape, 1 if keys_on_rows else 0)
    ki = lax.broadcasted_iota(jnp.int32, shape, 0 if keys_on_rows else 1)
    delta = offset0 + t * stride + ki - qi
    bucket = _rel_bucket(delta * dil)
    val = jnp.zeros(shape, F32)
    for j in range(N_BUCKETS):
        val = jnp.where(bucket == j, table_ref[j, col0 + h], val)
    if scale != 1.0:
        val = val * scale
    if band is not None:
        val = jnp.where(jnp.abs(delta) <= band, val, NEG)
    out_ref[0, 0] = val


def _bias_tiles(table, *, n_heads, n_tiles, tq, tk, col0, offset0, stride, dil, band, scale=1.0,
                keys_on_rows=False):
    shape = (tk, tq) if keys_on_rows else (tq, tk)
    return pl.pallas_call(
        functools.partial(_bias_tile_kernel, col0=col0, offset0=offset0, stride=stride, dil=dil,
                          band=band, scale=scale, keys_on_rows=keys_on_rows),
        grid=(n_heads, n_tiles),
        in_specs=[pl.BlockSpec(memory_space=pltpu.SMEM)],
        out_specs=pl.BlockSpec((1, 1) + shape, lambda h, t: (h, t, 0, 0)),
        out_shape=jax.ShapeDtypeStruct((n_heads, n_tiles) + shape, F32),
        compiler_params=_cparams("parallel", "parallel"),
        name="rel_bias_tiles",
    )(table)


def _mod_kernel(c_ref, w_ref, b_ref, o_ref):
    c = c_ref[...]
    a = (c * jax.nn.sigmoid(c)).astype(BF16)
    o_ref[0] = _dot(a, w_ref[0].astype(BF16)) + b_ref[0]


def _modulation(c_all, ada_w, ada_b):
    n, d = c_all.shape
    depth, _, width = ada_w.shape
    tn = 1536
    return pl.pallas_call(
        _mod_kernel,
        grid=(depth, width // tn),
        in_specs=[_const_spec((n, d)),
                  pl.BlockSpec((1, d, tn), lambda l, j: (l, 0, j)),
                  pl.BlockSpec((1, 1, tn), lambda l, j: (l, 0, j))],
        out_specs=pl.BlockSpec((1, n, tn), lambda l, j: (l, 0, j)),
        out_shape=jax.ShapeDtypeStruct((depth, n, width), F32),
        compiler_params=_cparams("parallel", "parallel"),
        name="adaln_modulation",
    )(c_all, ada_w, ada_b.reshape(depth, 1, width))


def _prenorm(x, g, shift, scale):
    return (_rms(x, g) * (1.0 + scale) + shift).astype(BF16)


def _even_in_kernel(x_ref, xp_ref, xn_ref, mod_ref, g_ref, wqk_ref, wvo_ref, wg_ref, wb_ref,
                    cw_ref, cb_ref, gb_ref,
                    qa_ref, ka_ref, va_ref, oa_ref, gates_ref, qb_ref, kb_ref, vb_ref, hx_ref, ext_ref):
    i = pl.program_id(1)
    n = pl.num_programs(1)
    tm = x_ref.shape[1]
    g = g_ref[...]
    shift = mod_ref[0, 0:1, :]
    scale = mod_ref[0, 1:2, :]
    h = _prenorm(x_ref[0], g, shift, scale)
    hx_ref[0:CONV_HALO, :] = _prenorm(xp_ref[0], g, shift, scale)
    hx_ref[CONV_HALO:CONV_HALO + tm, :] = h
    hx_ref[CONV_HALO + tm:, :] = _prenorm(xn_ref[0], g, shift, scale)
    ext_ref[...] = _dot(hx_ref[...], wqk_ref[...])
    ext_ref[0:CONV_HALO, :] = ext_ref[0:CONV_HALO, :] * (i > 0).astype(F32)
    ext_ref[CONV_HALO + tm:, :] = ext_ref[CONV_HALO + tm:, :] * (i < n - 1).astype(F32)
    acc = jnp.broadcast_to(cb_ref[...], (tm, 2 * A_WIDTH))
    for j in range(A_CONV):
        start = CONV_HALO - A_CONV // 2 + j
        acc = acc + cw_ref[j:j + 1, :] * ext_ref[start:start + tm, :]
    qk = acc * jax.nn.sigmoid(acc)
    qa_ref[0] = qk[:, :A_WIDTH].astype(BF16)
    ka_ref[0] = (qk[:, A_WIDTH:] * QK_SCALE).astype(BF16)
    vo = _dot(h, wvo_ref[...])
    va_ref[0] = vo[:, :A_WIDTH].astype(BF16)
    oa_ref[0] = vo[:, A_WIDTH:].astype(BF16)
    gates_ref[0] = _dot(h, wg_ref[...]) + gb_ref[...]
    qkv = _dot(h, wb_ref[...])
    qb_ref[0] = (qkv[:, :B_WIDTH] * (QK_SCALE * LOG2E)).astype(BF16)
    for hd in range(B_HEADS):
        kb_ref[0, hd] = qkv[:, B_WIDTH + hd * PAIR:B_WIDTH + (hd + 1) * PAIR].astype(BF16)
    vb_ref[0] = qkv[:, 2 * B_WIDTH:].astype(BF16)


def _even_in_proj(x, mod, g, w, conv_w, conv_b, gate_b):
    bsz, s, d = x.shape
    tm = min(TOKEN_TILE, s)
    nt = s // tm
    hb = tm // CONV_HALO
    n_halo = s // CONV_HALO
    tok = lambda width: pl.BlockSpec((1, tm, width), lambda b, i: (b, i, 0))
    outs = [A_WIDTH, A_WIDTH, A_WIDTH, A_WIDTH, N_GATES, B_WIDTH, B_WIDTH, B_WIDTH]
    dts = [BF16, BF16, BF16, BF16, F32, BF16, BF16, BF16]
    K_B_OUT = 6
    return pl.pallas_call(
        _even_in_kernel,
        grid=(bsz, nt),
        in_specs=[tok(d),
                  pl.BlockSpec((1, CONV_HALO, d), lambda b, i: (b, jnp.maximum(i * hb - 1, 0), 0)),
                  pl.BlockSpec((1, CONV_HALO, d),
                               lambda b, i: (b, jnp.minimum((i + 1) * hb, n_halo - 1), 0)),
                  pl.BlockSpec((1, 6, d), lambda b, i: (b, 0, 0)),
                  _const_spec((1, d)),
                  _const_spec(w["qk"].shape), _const_spec(w["vo"].shape),
                  _const_spec(w["g"].shape), _const_spec(w["b"].shape),
                  _const_spec(conv_w.shape), _const_spec(conv_b.shape), _const_spec(gate_b.shape)],
        out_specs=[pl.BlockSpec((1, B_HEADS, tm, PAIR), lambda b, i: (b, 0, i, 0)) if n == K_B_OUT
                   else tok(width) for n, width in enumerate(outs)],
        out_shape=[jax.ShapeDtypeStruct((bsz, B_HEADS, s, PAIR) if n == K_B_OUT else (bsz, s, width), dt)
                   for n, (width, dt) in enumerate(zip(outs, dts))],
        scratch_shapes=[pltpu.VMEM((tm + 2 * CONV_HALO, d), BF16),
                        pltpu.VMEM((tm + 2 * CONV_HALO, 2 * A_WIDTH), F32)],
        compiler_params=_cparams("parallel", "parallel"),
        name="even_in_proj",
    )(x, x, x, mod, g, w["qk"], w["vo"], w["g"], w["b"], conv_w, conv_b, gate_b)


def _odd_in_kernel(x_ref, mod_ref, g_ref, wc_ref, wd_ref, c0_ref, c1_ref, c2_ref, qd_ref, kd_ref, vd_ref,
                   slab_ref):
    h = _prenorm(x_ref[0], g_ref[...], mod_ref[0, 0:1, :], mod_ref[0, 1:2, :])
    tm = x_ref.shape[1]
    n_slabs = slab_ref.shape[0]
    for (_, dil), ref, g in zip(C_PAIRS, (c0_ref, c1_ref, c2_ref), range(len(C_PAIRS))):
        y = _dot(h, wc_ref[g])
        if dil == 1:
            ref[0] = y.astype(BF16)
            continue
        for c in range(n_slabs):
            slab_ref[c] = y[:, c * 128:(c + 1) * 128]
        for r in range(dil):
            for c in range(n_slabs):
                col = (r * n_slabs + c) * 128
                ref[0, :, col:col + 128] = slab_ref[c, pl.ds(r, tm // dil, stride=dil), :].astype(BF16)
    d = _dot(h, wd_ref[...])
    qd_ref[0] = d[:, :D_WIDTH].astype(BF16)
    kd_ref[0] = d[:, D_WIDTH:D_WIDTH + 2 * PAIR].astype(BF16)
    vd_ref[0] = d[:, D_WIDTH + 2 * PAIR:].astype(BF16)


def _odd_in_proj(x, mod, g, w):
    bsz, s, d = x.shape
    tm = min(TOKEN_TILE, s)
    tok = lambda width, dil=1: pl.BlockSpec((1, tm // dil, dil * width), lambda b, i: (b, i, 0))
    sds = lambda width, dil=1: jax.ShapeDtypeStruct((bsz, s // dil, dil * width), BF16)
    c_width = 3 * C_GW
    d_outs = [D_WIDTH, 2 * PAIR, 2 * PAIR]
    return pl.pallas_call(
        _odd_in_kernel,
        grid=(bsz, s // tm),
        in_specs=[tok(d), pl.BlockSpec((1, 6, d), lambda b, i: (b, 0, 0)), _const_spec((1, d)),
                  _const_spec(w["c"].shape), _const_spec(w["d"].shape)],
        out_specs=[tok(c_width, dil) for _, dil in C_PAIRS] + [tok(width) for width in d_outs],
        out_shape=[sds(c_width, dil) for _, dil in C_PAIRS] + [sds(width) for width in d_outs],
        scratch_shapes=[pltpu.VMEM((c_width // 128, tm, 128), F32)],
        compiler_params=_cparams("parallel", "parallel"),
        name="odd_in_proj",
    )(x, mod, g, w["c"], w["d"])


def _log_sigmoid(x):
    return jnp.minimum(x, 0.0) - jnp.log1p(jnp.exp(-jnp.abs(x)))


def _mlstm_kernel(*refs, reverse, n_chunks):
    if reverse:
        (q_ref, kt_ref, v_ref, gc_ref, gr_ref, hf_ref, oa_ref, ng_ref, out_ref, c_scr, m_scr) = refs
    else:
        (q_ref, kt_ref, v_ref, gc_ref, gr_ref, out_ref, c_scr, m_scr) = refs
    L = A_CHUNK

    @pl.when(pl.program_id(1) == 0)
    def _():
        c_scr[...] = jnp.zeros_like(c_scr)
        m_scr[...] = jnp.zeros_like(m_scr)

    row = lax.broadcasted_iota(jnp.int32, (L, L), 0)
    col = lax.broadcasted_iota(jnp.int32, (L, L), 1)
    causal = (col >= row) if reverse else (col <= row)
    causal_t = (row >= col) if reverse else (row <= col)
    causal_f = causal.astype(F32)
    causal_tf = causal_t.astype(F32)
    logf_c = _log_sigmoid(gc_ref[0]) * LOG2E
    logf_r = _log_sigmoid(gr_ref[0]) * LOG2E
    i_col0 = 2 * A_HEADS if reverse else 0
    f_col0 = i_col0 + A_HEADS
    last = 0 if reverse else L - 1
    low = lax.broadcasted_iota(jnp.int32, (L, PAIR), 1) < HEAD_DIM
    one = jnp.ones((L, PAIR), BF16)

    for c in (range(n_chunks - 1, -1, -1) if reverse else range(n_chunks)):
        rows = slice(c * L, (c + 1) * L)
        cum_c = jnp.dot(causal_f, logf_c[rows], precision=lax.Precision.HIGHEST,
                        preferred_element_type=F32)
        cum_r = jnp.dot(logf_r[:, rows], causal_tf, precision=lax.Precision.HIGHEST,
                        preferred_element_type=F32)
        r_all = (gr_ref[0, i_col0:i_col0 + A_HEADS, rows] * LOG2E
                 - cum_r[f_col0:f_col0 + A_HEADS, :])
        r_max_all = jnp.max(r_all, axis=1, keepdims=True)
        b_tot_all = cum_r[f_col0:f_col0 + A_HEADS, last:last + 1]
        m_prev_rows = m_scr[...]
        m_prev_all = m_prev_rows[:, 0:1]
        m_new_all = b_tot_all + jnp.maximum(m_prev_all, r_max_all)
        m_scr[...] = jnp.broadcast_to(m_new_all, m_scr.shape)
        decay_all = jnp.exp2(b_tot_all + m_prev_all - m_new_all)
        gain_all = jnp.exp2(b_tot_all + r_max_all - m_new_all)
        rp_all = jnp.maximum(r_all, m_prev_rows)
        for p in range(A_HEADS // 2):
            cols = slice(p * PAIR, (p + 1) * PAIR)
            q2 = q_ref[0, rows, cols]
            v2 = v_ref[0, rows, cols]
            kt2 = kt_ref[0, cols, rows]
            kt2f = kt2.astype(F32)
            tots, floors = [], []
            for half in range(2):
                hd = 2 * p + half
                mine = low if half == 0 else jnp.logical_not(low)
                qm = jnp.where(mine, q2, jnp.zeros_like(q2))
                vx = jnp.where(mine, v2, one)
                r_r = r_all[hd:hd + 1, :]
                b_c = cum_c[:, f_col0 + hd:f_col0 + hd + 1]
                c_ext = c_scr[hd]
                a = jnp.max(jnp.where(causal, rp_all[hd:hd + 1, :], -jnp.inf), axis=1, keepdims=True)
                sc = _dot(qm, kt2) * jnp.exp2(jnp.where(causal, r_r, -jnp.inf) - a)
                tots.append(_dot(sc.astype(BF16), vx)
                            + jnp.exp2(m_prev_rows[hd:hd + 1, :] - a) * _dot(qm, c_ext.astype(BF16)))
                floors.append(jnp.exp2(-(b_c + a)))
                upd = _dot((kt2f * jnp.exp2(r_r - r_max_all[hd:hd + 1, :])).astype(BF16), vx)
                c_scr[hd] = decay_all[hd:hd + 1, :] * c_ext + gain_all[hd:hd + 1, :] * upd
            den = pltpu.roll(jnp.where(low, tots[1], tots[0]), HEAD_DIM, axis=1)
            floor = jnp.where(low, floors[0], floors[1])
            h2 = jnp.where(low, tots[0], tots[1]) * (1.0 / jnp.maximum(jnp.abs(den), floor))
            if not reverse:
                out_ref[0, rows, cols] = h2
            else:
                hs = h2 + hf_ref[0, rows, cols]
                sq = hs * hs
                ms0 = jnp.sum(jnp.where(low, sq, 0.0), axis=1, keepdims=True) / HEAD_DIM
                ms1 = jnp.sum(jnp.where(low, 0.0, sq), axis=1, keepdims=True) / HEAD_DIM
                rs = jnp.where(low, lax.rsqrt(ms0 + EPS), lax.rsqrt(ms1 + EPS))
                y = hs * rs * ng_ref[:, cols]
                out_ref[0, rows, cols] = (jax.nn.sigmoid(oa_ref[0, rows, cols].astype(F32)) * y
                                          ).astype(BF16)


def _mlstm(q, kt, v, gates, gates_t, *, reverse, h_fwd=None, o_gate=None, norm_g=None):
    bsz, s, _ = q.shape
    n_chunks = min(A_CHUNKS_PER_STEP, s // A_CHUNK)
    tl = n_chunks * A_CHUNK
    nb = s // tl
    pos = (lambda c: nb - 1 - c) if reverse else (lambda c: c)
    tok = lambda width: pl.BlockSpec((1, tl, width), lambda b, c: (b, pos(c), 0))
    tok_t = lambda width: pl.BlockSpec((1, width, tl), lambda b, c: (b, 0, pos(c)))
    in_specs = [tok(A_WIDTH), tok_t(A_WIDTH), tok(A_WIDTH), tok(N_GATES), tok_t(N_GATES)]
    args = [q, kt, v, gates, gates_t]
    if reverse:
        in_specs += [tok(A_WIDTH), tok(A_WIDTH), _const_spec((1, A_WIDTH))]
        args += [h_fwd, o_gate, norm_g]
    return pl.pallas_call(
        functools.partial(_mlstm_kernel, reverse=reverse, n_chunks=n_chunks),
        grid=(bsz, nb),
        in_specs=in_specs,
        out_specs=tok(A_WIDTH),
        out_shape=jax.ShapeDtypeStruct((bsz, s, A_WIDTH), BF16 if reverse else F32),
        scratch_shapes=[pltpu.VMEM((A_HEADS, PAIR, PAIR), F32), pltpu.VMEM((A_HEADS, 128), F32)],
        compiler_params=_cparams("parallel", "arbitrary"),
        name="mlstm_bwd" if reverse else "mlstm_fwd",
    )(*args)


def _diff_tile_range(tq, tk):
    unit = math.gcd(tq, tk)
    e_hi = pl.cdiv(MAX_DISTANCE + tq - 1, unit)
    e_lo = -pl.cdiv(MAX_DISTANCE + tk - 1, unit)
    return unit, e_lo, e_hi


def _diff_kernel(lam_ref, qt_ref, k_ref, vt_ref, bias_ref, g_ref, o_ref,
                 s0, s1, p0, p1, a0, a1, x0, x1, m_scr, acc_scr, *, tk, unit, e_lo, e_hi, lam_init):
    tq = qt_ref.shape[2]
    nk = k_ref.shape[2] // tk
    q0 = pl.program_id(2) * tq
    qt = qt_ref[0]
    first = lax.broadcasted_iota(jnp.int32, qt.shape, 0) < HEAD_DIM
    zero = jnp.zeros_like(qt)
    q1t = jnp.where(first, qt, zero)
    q2t = jnp.where(first, zero, qt)
    ones_rows = jnp.ones((DIFF_ONES_ROWS, tk), BF16)
    m_scr[...] = jnp.full(m_scr.shape, -jnp.inf, F32)
    acc_scr[...] = jnp.zeros(acc_scr.shape, F32)

    def scores(kb, s_ref, x_ref):
        k0 = pl.multiple_of(kb * tk, tk)
        e = jnp.clip((k0 - q0) // unit, e_lo, e_hi) - e_lo
        k = k_ref[0, 0, pl.ds(k0, tk), :]
        for half, qh in enumerate((q1t, q2t)):
            sc = _dot(k, qh) + bias_ref[0, e]
            s_ref[:, half * tq:(half + 1) * tq] = sc
            x_ref[:, half * tq:(half + 1) * tq] = jnp.max(sc, axis=0, keepdims=True)

    def softmax(s_ref, x_ref, p_ref, a_ref):
        m_old = m_scr[...]
        m_new = jnp.maximum(m_old, x_ref[...])
        a_ref[...] = jnp.exp2(m_old - m_new)
        m_scr[...] = m_new
        p_ref[...] = jnp.exp2(s_ref[...] - m_new).astype(BF16)

    def accumulate(kb, p_ref, a_ref):
        k0 = pl.multiple_of(kb * tk, tk)
        vx = jnp.concatenate([vt_ref[0, :, pl.ds(k0, tk)], ones_rows], axis=0)
        acc_scr[...] = a_ref[...] * acc_scr[...] + _dot(vx, p_ref[...])

    scores(0, s0, x0)
    scores(1, s1, x1)
    softmax(s0, x0, p0, a0)

    def steady(j, carry):
        t = 2 * j + 1
        scores(t + 1, s0, x0)
        softmax(s1, x1, p1, a1)
        accumulate(t - 1, p0, a0)
        scores(t + 2, s1, x1)
        softmax(s0, x0, p0, a0)
        accumulate(t, p1, a1)
        return carry

    lax.fori_loop(0, (nk - 2) // 2, steady, 0)
    softmax(s1, x1, p1, a1)
    accumulate(nk - 2, p0, a0)
    accumulate(nk - 1, p1, a1)

    acc = acc_scr[...]
    o = acc[:PAIR] * (1.0 / acc[PAIR:PAIR + 1])
    lam = lam_ref[...]
    lam_full = (jnp.exp(jnp.sum(lam[0:1] * lam[1:2], axis=1, keepdims=True))
                - jnp.exp(jnp.sum(lam[2:3] * lam[3:4], axis=1, keepdims=True)) + lam_init)
    o = o[:, :tq] - lam_full * o[:, tq:]
    inv = lax.rsqrt(jnp.mean(o * o, axis=0, keepdims=True) + EPS)
    y = o * inv * g_ref[...] * (1.0 - lam_init)
    o_ref[0] = y.T.astype(BF16)


def _diff_attention(qt, k, vt, lam, g, bias_tiles, layer_idx):
    bsz, _, s, _ = k.shape
    tq, tk = min(DIFF_TQ, s), min(DIFF_TK, s)
    assert (s // tk) % 2 == 0, "the key-block pipeline is unrolled by two"
    unit, e_lo, e_hi = _diff_tile_range(tq, tk)
    n_tiles = e_hi - e_lo + 1
    lam_init = 0.8 - 0.6 * math.exp(-0.3 * layer_idx)
    acc_rows = PAIR + DIFF_ONES_ROWS
    return pl.pallas_call(
        functools.partial(_diff_kernel, tk=tk, unit=unit, e_lo=e_lo, e_hi=e_hi, lam_init=lam_init),
        grid=(B_HEADS, bsz, s // tq),
        in_specs=[_const_spec(lam.shape),
                  pl.BlockSpec((1, PAIR, tq), lambda h, b, i: (b, h, i)),
                  pl.BlockSpec((1, 1, s, PAIR), lambda h, b, i: (b, h, 0, 0)),
                  pl.BlockSpec((1, PAIR, s), lambda h, b, i: (b, h, 0)),
                  pl.BlockSpec((1, n_tiles, tk, tq), lambda h, b, i: (h, 0, 0, 0),
                               pipeline_mode=pl.Buffered(1)),
                  pl.BlockSpec((PAIR, 1), lambda h, b, i: (h, 0))],
        out_specs=pl.BlockSpec((1, tq, PAIR), lambda h, b, i: (b, i, h)),
        out_shape=jax.ShapeDtypeStruct((bsz, s, B_WIDTH), BF16),
        scratch_shapes=[pltpu.VMEM((tk, 2 * tq), F32), pltpu.VMEM((tk, 2 * tq), F32),
                        pltpu.VMEM((tk, 2 * tq), BF16), pltpu.VMEM((tk, 2 * tq), BF16),
                        pltpu.VMEM((1, 2 * tq), F32), pltpu.VMEM((1, 2 * tq), F32),
                        pltpu.VMEM((1, 2 * tq), F32), pltpu.VMEM((1, 2 * tq), F32),
                        pltpu.VMEM((1, 2 * tq), F32), pltpu.VMEM((acc_rows, 2 * tq), F32)],
        compiler_params=_cparams("parallel", "parallel", "arbitrary"),
        name="diff_attention",
    )(lam, qt, k, vt, bias_tiles, g)


def _diff_bias_tiles(table, s):
    tq, tk = min(DIFF_TQ, s), min(DIFF_TK, s)
    unit, e_lo, e_hi = _diff_tile_range(tq, tk)
    return _bias_tiles(table, n_heads=B_HEADS, n_tiles=e_hi - e_lo + 1, tq=tq, tk=tk, col0=B_COL0,
                       offset0=e_lo * unit, stride=unit, dil=1, band=None, scale=LOG2E,
                       keys_on_rows=True)


def _local_kernel(*refs, hw, ti, k_blocks, n_res, with_sink, with_lse):
    refs = list(refs)
    sink_ref = refs.pop(0) if with_sink else None
    (q_ref, kp_ref, kc_ref, kn_ref, vp_ref, vc_ref, vn_ref, bias_ref) = refs[:8]
    o_ref = refs[8]
    lse_ref = refs[9] if with_lse else None
    kext, vext = refs[-2:]
    tb = q_ref.shape[1]
    i = pl.program_id(1)
    res = pl.program_id(2)
    seq = pl.num_programs(1) * tb
    kext[0:hw, :] = kp_ref[0]
    kext[hw:hw + tb, :] = kc_ref[0]
    kext[hw + tb:, :] = kn_ref[0]
    vext[0:hw, :] = vp_ref[0]
    vext[hw:hw + tb, :] = vc_ref[0]
    vext[hw + tb:, :] = vn_ref[0]
    wk = ti + 2 * hw
    n_sub = tb // ti
    low = lax.broadcasted_iota(jnp.int32, (ti, PAIR), 1) < HEAD_DIM
    kj = lax.broadcasted_iota(jnp.int32, (1, wk), 1)
    for j in range(n_sub):
        rows = slice(j * ti, (j + 1) * ti)
        krows = slice(j * ti, j * ti + wk)
        out_rows = rows if n_res == 1 else pl.ds(j * ti * n_res + res, ti, stride=n_res)
        edge = None
        if j in (0, n_sub - 1):
            kpos = i * tb + j * ti - hw + kj
            edge = jnp.where((kpos >= 0) & (kpos < seq), 0.0, NEG)
        for kb in sorted(set(k_blocks)):
            pairs = [p for p, b in enumerate(k_blocks) if b == kb]
            kk = kext[krows, kb * PAIR:(kb + 1) * PAIR]
            vv = vext[krows, kb * PAIR:(kb + 1) * PAIR]
            stack = []
            for p in pairs:
                q2 = q_ref[0, rows, p * PAIR:(p + 1) * PAIR]
                zero = jnp.zeros_like(q2)
                stack += [jnp.where(low, q2, zero), jnp.where(low, zero, q2)]
            sc_all = _dot_nt(jnp.concatenate(stack, axis=0), kk)
            probs, inv_dens, lses = [], [], []
            for n, p in enumerate(pairs):
                for half in range(2):
                    hd = 2 * p + half
                    sc = sc_all[(2 * n + half) * ti:(2 * n + half + 1) * ti] + bias_ref[hd]
                    if edge is not None:
                        sc = sc + edge
                    m = jnp.max(sc, axis=1, keepdims=True)
                    if with_sink:
                        sink = sink_ref[hd] * LOG2E
                        m = jnp.maximum(m, sink)
                    pr = jnp.exp2(sc - m)
                    den = jnp.sum(pr, axis=1, keepdims=True)
                    if with_sink:
                        den = den + jnp.exp2(sink - m)
                    probs.append(pr.astype(BF16))
                    inv_dens.append(1.0 / den)
                    lses.append(m * LN2 + jnp.log(den))
            o_all = _dot(jnp.concatenate(probs, axis=0), vv)
            for n, p in enumerate(pairs):
                o0 = o_all[2 * n * ti:(2 * n + 1) * ti] * inv_dens[2 * n]
                o1 = o_all[(2 * n + 1) * ti:(2 * n + 2) * ti] * inv_dens[2 * n + 1]
                o_ref[0, p, out_rows, :] = jnp.where(low, o0, o1).astype(o_ref.dtype)
                if with_lse:
                    lse_ref[0, p, out_rows, :] = jnp.where(low, lses[2 * n], lses[2 * n + 1])


def _local_tiling(seq, n_res):
    tb = min(LOCAL_OUTER, seq, LOCAL_OUT_ROWS // n_res)
    return tb, min(LOCAL_TILE, tb)


def _local_attention(q_arr, k_arr, v_arr, bias, *, seq, n_res, q_width, q_col, k_col, v_col, hw,
                     k_blocks, out_dtype, sink=None, with_lse=False):
    bsz = q_arr.shape[0]
    tb, ti = _local_tiling(seq, n_res)
    kw = len(set(k_blocks)) * PAIR
    hb = tb // hw
    n_halo = seq // hw
    cur = lambda width, colf: pl.BlockSpec((1, tb, width), lambda b, i, r: (b, i, colf(r)))
    prev = lambda colf: pl.BlockSpec((1, hw, kw), lambda b, i, r: (b, jnp.maximum(i * hb - 1, 0), colf(r)))
    nxt = lambda colf: pl.BlockSpec(
        (1, hw, kw), lambda b, i, r: (b, jnp.minimum((i + 1) * hb, n_halo - 1), colf(r)))
    in_specs = [cur(q_width, q_col), prev(k_col), cur(kw, k_col), nxt(k_col),
                prev(v_col), cur(kw, v_col), nxt(v_col), _const_spec(bias.shape)]
    args = [q_arr, k_arr, k_arr, k_arr, v_arr, v_arr, v_arr, bias]
    if sink is not None:
        in_specs.insert(0, pl.BlockSpec(memory_space=pltpu.SMEM))
        args.insert(0, sink)
    n_pairs = q_width // PAIR
    out_spec = pl.BlockSpec((1, n_pairs, tb * n_res, PAIR), lambda b, i, r: (b, 0, i, 0))
    out_specs = [out_spec]
    out_shape = [jax.ShapeDtypeStruct((bsz, n_pairs, seq * n_res, PAIR), out_dtype)]
    if with_lse:
        out_specs.append(out_spec)
        out_shape.append(jax.ShapeDtypeStruct((bsz, n_pairs, seq * n_res, PAIR), F32))
    return pl.pallas_call(
        functools.partial(_local_kernel, hw=hw, ti=ti, k_blocks=tuple(k_blocks), n_res=n_res,
                          with_sink=sink is not None, with_lse=with_lse),
        grid=(bsz, seq // tb, n_res),
        in_specs=in_specs,
        out_specs=out_specs,
        out_shape=out_shape,
        scratch_shapes=[pltpu.VMEM((tb + 2 * hw, kw), BF16), pltpu.VMEM((tb + 2 * hw, kw), BF16)],
        compiler_params=_cparams("parallel", "parallel", "arbitrary"),
        name="local_attention",
    )(*args)


def _post_kernel(*refs, even):
    refs = list(refs)
    if even:
        a_ref, b_ref = refs[:2]
        rest = refs[2:]
    else:
        c_refs = refs[:6]
        d_ref = refs[6]
        rest = refs[7:]
    x_ref, mod_ref, gains_ref, wo_ref, w1_ref, w2_ref, out_ref = rest
    if even:
        y = _dot(a_ref[0], wo_ref[0:A_WIDTH, :]) + _dot(b_ref[0], wo_ref[A_WIDTH:, :])
    else:
        scaled = [[], [], []]
        for p in range(C_GW // PAIR):
            ls_ = [c_refs[2 * g + 1][0, p] for g in range(3)]
            mx = jnp.maximum(jnp.maximum(ls_[0], ls_[1]), ls_[2])
            es = [jnp.exp(l - mx) for l in ls_]
            inv = 1.0 / (es[0] + es[1] + es[2])
            for g in range(3):
                scaled[g].append((c_refs[2 * g][0, p] * (es[g] * inv)).astype(BF16))
        y = _dot(jnp.concatenate([d_ref[0, p] for p in range(D_WIDTH // PAIR)], axis=1),
                 wo_ref[C_WIDTH:, :])
        for g in range(3):
            y = y + _dot(jnp.concatenate(scaled[g], axis=1), wo_ref[g * C_GW:(g + 1) * C_GW, :])
    gate1 = mod_ref[0, 2:3, :]
    shift2 = mod_ref[0, 3:4, :]
    scale2 = mod_ref[0, 4:5, :]
    gate2 = mod_ref[0, 5:6, :]
    x = x_ref[0] + gate1 * _rms(y, gains_ref[0:1, :])
    h = _prenorm(x, gains_ref[1:2, :], shift2, scale2)
    acc = jnp.zeros_like(x)
    step = D_MODEL
    for j in range(D_FF // step):
        u = jnp.maximum(_dot(h, w1_ref[:, j * step:(j + 1) * step]), 0.0)
        acc = acc + _dot((u * u).astype(BF16), w2_ref[j * step:(j + 1) * step, :])
    out_ref[0] = x + gate2 * _rms(acc, gains_ref[2:3, :])


def _post_mixer(mix, x, mod, gains, wo, w1, w2, *, even):
    bsz, s, d = x.shape
    tm = min(TOKEN_TILE, s)
    tok = lambda width: pl.BlockSpec((1, tm, width), lambda b, i: (b, i, 0))
    resident = lambda a: pl.BlockSpec(a.shape, lambda b, i: (0, 0), pipeline_mode=pl.Buffered(1))
    slabs = lambda n: pl.BlockSpec((1, n, tm, PAIR), lambda b, i: (b, 0, i, 0))
    in_specs = [tok(m.shape[-1]) if m.ndim == 3 else slabs(m.shape[1]) for m in mix]
    in_specs += [tok(d), pl.BlockSpec((1, 6, d), lambda b, i: (b, 0, 0)), _const_spec(gains.shape),
                 resident(wo), resident(w1), resident(w2)]
    return pl.pallas_call(
        functools.partial(_post_kernel, even=even),
        grid=(bsz, s // tm),
        in_specs=in_specs,
        out_specs=tok(d),
        out_shape=jax.ShapeDtypeStruct((bsz, s, d), F32),
        compiler_params=_cparams("parallel", "parallel"),
        name="post_mixer_mlp",
    )(*mix, x, mod, gains, wo, w1, w2)


def _even_layer(x, mod, layer_idx, p, diff_tiles):
    e = layer_idx // 2
    w_in = p["even_w_in"][e]
    o1 = 2 * A_WIDTH
    o2 = o1 + 2 * A_WIDTH
    o3 = o2 + N_GATES
    w = {"qk": w_in[:, :o1].astype(BF16), "vo": w_in[:, o1:o2].astype(BF16),
         "g": w_in[:, o2:o3].astype(BF16), "b": w_in[:, o3:].astype(BF16)}
    qa, ka, va, oa, gates, qb, kb, vb = _even_in_proj(
        x, mod, p["mix_pre_g"][layer_idx][None], w, p["even_conv_w"][e],
        p["even_conv_b"][e][None], p["mlstm_gate_b"][e].reshape(1, N_GATES))
    gates_t = jnp.swapaxes(gates, 1, 2)
    ka_t = jnp.swapaxes(ka, 1, 2)
    h_fwd = _mlstm(qa, ka_t, va, gates, gates_t, reverse=False)
    mix_a = _mlstm(qa, ka_t, va, gates, gates_t, reverse=True, h_fwd=h_fwd, o_gate=oa,
                   norm_g=p["mlstm_norm_g"][e][None])
    mix_b = _diff_attention(jnp.swapaxes(qb, 1, 2), kb, jnp.swapaxes(vb, 1, 2), p["diff_lambda"][e],
                            p["diff_norm_g"][e][:, None], diff_tiles, layer_idx)
    return [mix_a, mix_b], p["even_w_out"][e].astype(BF16)


def _odd_layer(x, mod, layer_idx, p, c_tiles, d_tiles):
    o = layer_idx // 2
    bsz, s, _ = x.shape
    w_in = p["odd_w_in"][o]
    cw = C_WIDTH
    wq, wk, wv = w_in[:, :cw] * (QK_SCALE * LOG2E), w_in[:, cw:2 * cw], w_in[:, 2 * cw:3 * cw]
    wc = jnp.stack([jnp.concatenate([t[:, g * C_GW:(g + 1) * C_GW] for t in (wq, wk, wv)], axis=1)
                    for g in range(len(C_PAIRS))]).astype(BF16)
    d0 = 3 * cw
    wqd = w_in[:, d0:d0 + D_WIDTH] * (QK_SCALE * LOG2E)
    wkd = w_in[:, d0 + D_WIDTH:d0 + D_WIDTH + D_KV_HEADS * HEAD_DIM]
    wvd = w_in[:, d0 + D_WIDTH + D_KV_HEADS * HEAD_DIM:]
    dup = lambda t: jnp.concatenate([t[:, g * HEAD_DIM:(g + 1) * HEAD_DIM]
                                     for g in range(D_KV_HEADS) for _ in range(2)], axis=1)
    wd = jnp.concatenate([wqd, dup(wkd), dup(wvd)], axis=1).astype(BF16)
    c0, c1, c2, qd, kd, vd = _odd_in_proj(x, mod, p["mix_pre_g"][layer_idx][None], {"c": wc, "d": wd})
    mix = []
    for g, (strided, (window, dil)) in enumerate(zip((c0, c1, c2), C_PAIRS)):
        sub = s // dil
        o_g, lse_g = _local_attention(
            strided, strided, strided, c_tiles[g], seq=sub, n_res=dil, q_width=C_GW,
            q_col=lambda r: 3 * r, k_col=lambda r: 3 * r + 1, v_col=lambda r: 3 * r + 2,
            hw=window // (2 * dil), k_blocks=(0, 1), with_lse=True,
            out_dtype=BF16 if dil == 1 else F32)
        mix += [o_g, lse_g]
    o_d = _local_attention(qd, kd, vd, d_tiles, seq=s, n_res=1, q_width=D_WIDTH,
                           q_col=lambda r: 0, k_col=lambda r: 0, v_col=lambda r: 0,
                           hw=D_HALF_WINDOW, k_blocks=(0, 0, 1, 1), out_dtype=BF16,
                           sink=p["sink_logit"][o])[0]
    mix.append(o_d)
    return mix, p["odd_w_out"][o].astype(BF16)


def _local_bias_tiles(table, seq, *, n_heads, col0, hw, dil):
    _, ti = _local_tiling(seq, dil)
    return _bias_tiles(table, n_heads=n_heads, n_tiles=1, tq=ti, tk=ti + 2 * hw, col0=col0,
                       offset0=-hw, stride=0, dil=dil, band=hw, scale=LOG2E)[:, 0]


def _trunk(x, mods, p):
    bsz, s, _ = x.shape
    table = p["rel_bias_table"]
    diff_tiles = _diff_bias_tiles(table, s)
    c_tiles = [_local_bias_tiles(table, s // dil, n_heads=C_HPG, col0=C_COL0 + g * C_HPG,
                                 hw=window // (2 * dil), dil=dil)
               for g, (window, dil) in enumerate(C_PAIRS)]
    d_tiles = _local_bias_tiles(table, s, n_heads=D_HEADS, col0=D_COL0, hw=D_HALF_WINDOW, dil=1)
    for l in range(DEPTH):
        mod = mods[l].reshape(bsz, 6, D_MODEL)
        if l % 2 == 0:
            mix, wo = _even_layer(x, mod, l, p, diff_tiles)
        else:
            mix, wo = _odd_layer(x, mod, l, p, c_tiles, d_tiles)
        gains = jnp.stack([p["mix_post_g"][l], p["mlp_pre_g"][l], p["mlp_post_g"][l]])
        x = _post_mixer(mix, x, mod, gains, wo, p["mlp_w1"][l].astype(BF16),
                        p["mlp_w2"][l].astype(BF16), even=l % 2 == 0)
    return x


def kernel(x_prompt, x_sample, c_prompt, c_sample, rel_bias_table, ada_w, ada_b, mix_pre_g, mix_post_g, mlp_pre_g, mlp_post_g, mlp_w1, mlp_w2, even_w_in, even_conv_w, even_conv_b, mlstm_gate_b, mlstm_norm_g, diff_lambda, diff_norm_g, even_w_out, odd_w_in, odd_w_out, sink_logit):
    p = dict(rel_bias_table=rel_bias_table, mix_pre_g=mix_pre_g, mix_post_g=mix_post_g,
             mlp_pre_g=mlp_pre_g, mlp_post_g=mlp_post_g, mlp_w1=mlp_w1, mlp_w2=mlp_w2,
             even_w_in=even_w_in, even_conv_w=even_conv_w, even_conv_b=even_conv_b,
             mlstm_gate_b=mlstm_gate_b, mlstm_norm_g=mlstm_norm_g, diff_lambda=diff_lambda,
             diff_norm_g=diff_norm_g, even_w_out=even_w_out, odd_w_in=odd_w_in,
             odd_w_out=odd_w_out, sink_logit=sink_logit)
    n_prompt = c_prompt.shape[0]
    mods = _modulation(jnp.concatenate([c_prompt, c_sample], axis=0), ada_w, ada_b)
    y_prompt = _trunk(x_prompt, mods[:, :n_prompt], p)
    y_sample = _trunk(x_sample, mods[:, n_prompt:], p)
    return (y_prompt, y_sample)
```
